```python
import math
import jax, jax.numpy as jnp
from jax import lax
import numpy as np

D_MODEL = 2048
BATCH = 16
SEQ = 2048
DEPTH = 2
DEC_BATCH = 4
DEC_SEQ = 2048
PAST_LEN = 128

GRID_W = 64
N_MEM = 256
EPS = 1e-6
ROPE_THETA = 10000.0
BLOCK_Q = 128

RET_HEADS = 4
RET_DK = 256
RET_DV = 256
RET_CHUNK = 128
RET_DECAY_EXP_FWD = 5.0
RET_DECAY_EXP_BWD = 5.5
ATT_HEADS = 8
ATT_KV_HEADS = 2
ATT_HD = 128
EVEN_SPLITS = (RET_HEADS * RET_DK, RET_HEADS * RET_DK, RET_HEADS * RET_DV, RET_HEADS * RET_DV,
               ATT_HEADS * ATT_HD, ATT_KV_HEADS * ATT_HD, ATT_KV_HEADS * ATT_HD)
EVEN_IN = 5632
EVEN_OUT = RET_HEADS * RET_DV + ATT_HEADS * ATT_HD
SSD_EXPAND = 2
SSD_INNER = SSD_EXPAND * D_MODEL
SSD_HEADDIM = 64
SSD_HEADS = SSD_INNER // SSD_HEADDIM
SSD_STATE = 128
SSD_GROUPS = 8
SSD_CONV = 5
SSD_CHUNK = 128
SSD_CONV_DIM = SSD_INNER + 2 * SSD_GROUPS * SSD_STATE
ODD_IN = SSD_INNER + SSD_CONV_DIM + 2 * SSD_HEADS
XA_HEADS = 4
XA_HD = D_MODEL // XA_HEADS
D_FF = ((8 * D_MODEL // 3 + 255) // 256) * 256
N_NORMS = 7
N_EVEN = (DEPTH + 1) // 2
N_ODD = DEPTH // 2

kernel_name = 'hybrid_retention_gqa_ssd_encoder'


def rms_norm(x, g):
    xf = x.astype(jnp.float32)
    y = xf * lax.rsqrt(jnp.mean(xf * xf, axis=-1, keepdims=True) + EPS)
    return (y * g.astype(jnp.float32)).astype(x.dtype)


def rope_angles(pos, dim):
    inv = ROPE_THETA ** (-jnp.arange(0, dim, 2, dtype=jnp.float32) / dim)
    ang = pos.astype(jnp.float32)[:, None] * inv[None, :]
    return jnp.cos(ang), jnp.sin(ang)


def apply_rope(x, cos, sin):
    xf = x.astype(jnp.float32)
    x1, x2 = jnp.split(xf, 2, axis=-1)
    c, s = cos[:, None, :], sin[:, None, :]
    return jnp.concatenate([x1 * c - x2 * s, x2 * c + x1 * s], axis=-1).astype(x.dtype)


def retention_scan(q, k, v, log_gamma, strict):
    b, h, t, dk = q.shape
    dv = v.shape[-1]
    c = RET_CHUNK
    n = t // c
    idx = jnp.arange(c, dtype=jnp.float32)
    diff = idx[:, None] - idx[None, :]
    mask = (diff > 0) if strict else (diff >= 0)
    lg = log_gamma[:, None, None]
    dmat = jnp.where(mask, jnp.exp(lg * jnp.where(mask, diff, 0.0)), 0.0)
    xi = jnp.exp(lg * (idx + 1.0)[None, :, None])
    zeta = jnp.exp(lg * (c - 1.0 - idx)[None, :, None])
    g_chunk = jnp.exp(lg * c)

    def step(state, inp):
        qc, kc, vc = inp
        inner = jnp.einsum('bhij,bhjv->bhiv', jnp.einsum('bhid,bhjd->bhij', qc, kc) * dmat, vc)
        cross = jnp.einsum('bhid,bhdv->bhiv', qc * xi, state)
        state = g_chunk * state + jnp.einsum('bhjd,bhjv->bhdv', kc, vc * zeta)
        return state, inner + cross

    to_chunks = lambda a: jnp.moveaxis(a.reshape(b, h, n, c, a.shape[-1]), 2, 0)
    state0 = jnp.zeros((b, h, dk, dv), jnp.float32)
    _, out = lax.scan(step, state0, (to_chunks(q), to_chunks(k), to_chunks(v)))
    return jnp.moveaxis(out, 0, 2).reshape(b, h, t, dv)


def retention_attention_mixer(h, w_in, q_gain, k_gain, w_out):
    b, t, _ = h.shape
    proj = h @ w_in
    offs = np.cumsum(EVEN_SPLITS)[:-1].tolist()
    rq, rk, rv, rg, aq, ak, av = jnp.split(proj, offs, axis=-1)

    pos = jnp.arange(t)
    cos1, sin1 = rope_angles(pos, RET_DK)
    rq = apply_rope(rq.reshape(b, t, RET_HEADS, RET_DK), cos1, sin1)
    rk = apply_rope(rk.reshape(b, t, RET_HEADS, RET_DK), cos1, sin1) * (RET_DK ** -0.5)
    to_bhtd = lambda a: a.transpose(0, 2, 1, 3).astype(jnp.float32)
    rq, rk = to_bhtd(rq), to_bhtd(rk)
    rv = to_bhtd(rv.reshape(b, t, RET_HEADS, RET_DV))
    heads = jnp.arange(RET_HEADS, dtype=jnp.float32)
    lg_f = jnp.log1p(-jnp.exp2(-(RET_DECAY_EXP_FWD + heads)))
    lg_b = jnp.log1p(-jnp.exp2(-(RET_DECAY_EXP_BWD + heads)))
    flip = lambda a: jnp.flip(a, axis=2)
    ret = (retention_scan(rq, rk, rv, lg_f, False)
           + flip(retention_scan(flip(rq), flip(rk), flip(rv), lg_b, True)))
    mu = jnp.mean(ret, axis=-1, keepdims=True)
    var = jnp.mean(jnp.square(ret - mu), axis=-1, keepdims=True)
    ret = ((ret - mu) * lax.rsqrt(var + EPS)).transpose(0, 2, 1, 3).reshape(b, t, RET_HEADS * RET_DV)
    ret = (jax.nn.silu(rg.astype(jnp.float32)) * ret).astype(h.dtype)

    aq = rms_norm(aq.reshape(b, t, ATT_HEADS, ATT_HD), q_gain)
    ak = rms_norm(ak.reshape(b, t, ATT_KV_HEADS, ATT_HD), k_gain)
    av = av.reshape(b, t, ATT_KV_HEADS, ATT_HD)
    rows = t // GRID_W
    row = jnp.repeat(jnp.arange(rows), GRID_W)
    col = jnp.tile(jnp.arange(GRID_W), rows)
    half = ATT_HD // 2
    cos_r, sin_r = rope_angles(row, half)
    cos_c, sin_c = rope_angles(col, half)
    axial = lambda a: jnp.concatenate([apply_rope(a[..., :half], cos_r, sin_r),
                                       apply_rope(a[..., half:], cos_c, sin_c)], axis=-1)
    aq, ak = axial(aq), axial(ak)
    grp = ATT_HEADS // ATT_KV_HEADS
    nb = t // BLOCK_Q
    qb = aq.reshape(b, nb, BLOCK_Q, ATT_KV_HEADS, grp, ATT_HD).transpose(1, 0, 3, 4, 2, 5)
    kh = ak.transpose(0, 2, 1, 3)
    vh = av.transpose(0, 2, 1, 3)

    def attend(qblk):
        s = jnp.einsum('bkgqd,bktd->bkgqt', qblk, kh).astype(jnp.float32) * (ATT_HD ** -0.5)
        p = jax.nn.softmax(s, axis=-1).astype(vh.dtype)
        return jnp.einsum('bkgqt,bktd->bkgqd', p, vh)

    att = lax.map(attend, qb)
    att = att.transpose(1, 0, 4, 2, 3, 5).reshape(b, t, ATT_HEADS * ATT_HD)
    return jnp.concatenate([ret, att], axis=-1) @ w_out


def ssd_scan(x, dt, a, bm, cm):
    b, t, nh, p = x.shape
    g, n = bm.shape[2], bm.shape[3]
    kh = nh // g
    L = SSD_CHUNK
    nc = t // L
    chunks = lambda arr, tail: jnp.moveaxis(arr.reshape((b, nc, L) + tail), 1, 0)
    xs = chunks(x * dt[..., None], (g, kh, p))
    a_s = chunks(dt * a, (g, kh))
    bs = chunks(bm, (g, n))
    cs = chunks(cm, (g, n))
    causal = (jnp.arange(L)[:, None] >= jnp.arange(L)[None, :])[None, :, :, None, None]

    def step(state, inp):
        xc, ac, bc, cc = inp
        acum = jnp.cumsum(ac, axis=1)
        seg = acum[:, :, None] - acum[:, None, :]
        lmat = jnp.where(causal, jnp.exp(jnp.where(causal, seg, 0.0)), 0.0)
        cb = jnp.einsum('blgn,bsgn->blsg', cc, bc)
        y_diag = jnp.einsum('blsgk,bsgkp->blgkp', cb[..., None] * lmat, xc)
        y_off = jnp.einsum('blgn,bgkpn->blgkp', cc, state) * jnp.exp(acum)[..., None]
        decay = jnp.exp(acum[:, -1:] - acum)[..., None]
        state = (state * jnp.exp(acum[:, -1])[..., None, None]
                 + jnp.einsum('blgn,blgkp->bgkpn', bc, decay * xc))
        return state, y_diag + y_off

    state0 = jnp.zeros((b, g, kh, p, n), jnp.float32)
    _, y = lax.scan(step, state0, (xs, a_s, bs, cs))
    return jnp.moveaxis(y, 0, 1).reshape(b, t, nh, p)


def ssd_mixer(h, w_in, conv_w, conv_b, a_log, dt_bias, d_skip, norm_g, w_out):
    b, t, _ = h.shape
    proj = h @ w_in
    z, xbc, dt = jnp.split(proj, [SSD_INNER, SSD_INNER + SSD_CONV_DIM], axis=-1)
    pad = SSD_CONV // 2
    xbc = lax.conv_general_dilated(xbc, conv_w[:, None, :], window_strides=(1,), padding=[(pad, pad)],
                                   dimension_numbers=('NWC', 'WIO', 'NWC'),
                                   feature_group_count=SSD_CONV_DIM)
    xbc = jax.nn.silu((xbc + conv_b).astype(jnp.float32))
    xs, bm, cm = jnp.split(xbc, [SSD_INNER, SSD_INNER + SSD_GROUPS * SSD_STATE], axis=-1)
    xs = xs.reshape(b, t, SSD_HEADS, SSD_HEADDIM)
    bm = bm.reshape(b, t, SSD_GROUPS, SSD_STATE)
    cm = cm.reshape(b, t, SSD_GROUPS, SSD_STATE)
    dt = jax.nn.softplus(dt.reshape(b, t, 2, SSD_HEADS).astype(jnp.float32) + dt_bias.astype(jnp.float32))
    a = -jnp.exp(a_log.astype(jnp.float32))
    flip = lambda arr: jnp.flip(arr, axis=1)
    y = (ssd_scan(xs, dt[:, :, 0], a[0], bm, cm)
         + flip(ssd_scan(flip(xs), flip(dt[:, :, 1]), a[1], flip(bm), flip(cm)))
         + d_skip.astype(jnp.float32)[:, None] * xs)
    y = y.reshape(b, t, SSD_INNER) * jax.nn.silu(z.astype(jnp.float32))
    yg = y.reshape(b, t, SSD_GROUPS, SSD_INNER // SSD_GROUPS)
    yg = yg * lax.rsqrt(jnp.mean(yg * yg, axis=-1, keepdims=True) + EPS)
    y = (yg.reshape(b, t, SSD_INNER) * norm_g.astype(jnp.float32)).astype(h.dtype)
    return y @ w_out


def memory_cross_attention(h, m, wq, wkv, wo):
    b, t, _ = h.shape
    q = (h @ wq).reshape(b, t, XA_HEADS, XA_HD)
    kv = (m @ wkv).reshape(b, m.shape[1], 2, XA_HEADS, XA_HD)
    s = jnp.einsum('bthd,bmhd->bhtm', q, kv[:, :, 0]).astype(jnp.float32) * (XA_HD ** -0.5)
    p = jax.nn.softmax(s, axis=-1).astype(h.dtype)
    o = jnp.einsum('bhtm,bmhd->bthd', p, kv[:, :, 1]).reshape(b, t, D_MODEL)
    return o @ wo


def swiglu(h, w_gu, w_down):
    g, u = jnp.split(h @ w_gu, 2, axis=-1)
    return (jax.nn.silu(g) * u) @ w_down


def encoder_trunk(x, mem, norm_g, ev_w_in, ev_q_gain, ev_k_gain, ev_w_out,
                  od_w_in, od_conv_w, od_conv_b, od_a_log, od_dt_bias, od_d, od_norm_g, od_w_out,
                  xa_wq, xa_wkv, xa_wo, ffn_w_gu, ffn_w_down):
    for i in range(DEPTH):
        gn = norm_g[i]
        j = i // 2
        hin = rms_norm(x, gn[0])
        if i % 2 == 0:
            mix = retention_attention_mixer(hin, ev_w_in[j], ev_q_gain[j], ev_k_gain[j], ev_w_out[j])
        else:
            mix = ssd_mixer(hin, od_w_in[j], od_conv_w[j], od_conv_b[j], od_a_log[j], od_dt_bias[j],
                            od_d[j], od_norm_g[j], od_w_out[j])
        x = x + rms_norm(mix, gn[1])
        xa = memory_cross_attention(rms_norm(x, gn[2]), rms_norm(mem, gn[4]), xa_wq[i], xa_wkv[i], xa_wo[i])
        x = x + rms_norm(xa, gn[3])
        ff = swiglu(rms_norm(x, gn[5]), ffn_w_gu[i], ffn_w_down[i])
        x = x + rms_norm(ff, gn[6])
    return x


def setup_inputs(seed: int = 0) -> dict:
    key = jax.random.key(seed)
    ks = jax.random.split(key, 24)
    f32 = jnp.float32
    nrm = lambda k, shape, scale: scale * jax.random.normal(k, shape, f32)
    gain = lambda k, shape: 1.0 + 0.02 * jax.random.normal(k, shape, f32)
    dt0 = jnp.exp(jax.random.uniform(ks[13], (N_ODD, 2, SSD_HEADS), f32, math.log(1e-3), math.log(1e-1)))
    return {
        'x_prompt': nrm(ks[0], (BATCH, SEQ, D_MODEL), 1.0),
        'x_sample': nrm(ks[1], (DEC_BATCH, DEC_SEQ, D_MODEL), 1.0),
        'mem_prompt': nrm(ks[2], (BATCH, N_MEM, D_MODEL), 1.0),
        'mem_sample': nrm(ks[3], (DEC_BATCH, N_MEM, D_MODEL), 1.0),
        'norm_g': gain(ks[4], (DEPTH, N_NORMS, D_MODEL)),
        'ev_w_in': nrm(ks[5], (N_EVEN, D_MODEL, EVEN_IN), D_MODEL ** -0.5),
        'ev_q_gain': gain(ks[6], (N_EVEN, ATT_HD)),
        'ev_k_gain': gain(ks[7], (N_EVEN, ATT_HD)),
        'ev_w_out': nrm(ks[8], (N_EVEN, EVEN_OUT, D_MODEL), EVEN_OUT ** -0.5),
        'od_w_in': nrm(ks[9], (N_ODD, D_MODEL, ODD_IN), D_MODEL ** -0.5),
        'od_conv_w': nrm(ks[10], (N_ODD, SSD_CONV, SSD_CONV_DIM), SSD_CONV ** -0.5),
        'od_conv_b': nrm(ks[11], (N_ODD, SSD_CONV_DIM), 0.01),
        'od_a_log': jnp.log(jax.random.uniform(ks[12], (N_ODD, 2, SSD_HEADS), f32, 1.0, 16.0)),
        'od_dt_bias': dt0 + jnp.log(-jnp.expm1(-dt0)),
        'od_d': gain(ks[14], (N_ODD, SSD_HEADS)),
        'od_norm_g': gain(ks[15], (N_ODD, SSD_INNER)),
        'od_w_out': nrm(ks[16], (N_ODD, SSD_INNER, D_MODEL), SSD_INNER ** -0.5),
        'xa_wq': nrm(ks[17], (DEPTH, D_MODEL, D_MODEL), D_MODEL ** -0.5),
        'xa_wkv': nrm(ks[18], (DEPTH, D_MODEL, 2 * D_MODEL), D_MODEL ** -0.5),
        'xa_wo': nrm(ks[19], (DEPTH, D_MODEL, D_MODEL), D_MODEL ** -0.5),
        'ffn_w_gu': nrm(ks[20], (DEPTH, D_MODEL, 2 * D_FF), D_MODEL ** -0.5),
        'ffn_w_down': nrm(ks[21], (DEPTH, D_FF, D_MODEL), D_FF ** -0.5),
    }


def reference(x_prompt, x_sample, mem_prompt, mem_sample, norm_g, ev_w_in, ev_q_gain, ev_k_gain, ev_w_out,
              od_w_in, od_conv_w, od_conv_b, od_a_log, od_dt_bias, od_d, od_norm_g, od_w_out,
              xa_wq, xa_wkv, xa_wo, ffn_w_gu, ffn_w_down):
    y_prompt = encoder_trunk(x_prompt, mem_prompt, norm_g, ev_w_in, ev_q_gain, ev_k_gain, ev_w_out,
                             od_w_in, od_conv_w, od_conv_b, od_a_log, od_dt_bias, od_d, od_norm_g, od_w_out,
                             xa_wq, xa_wkv, xa_wo, ffn_w_gu, ffn_w_down)
    y_sample = encoder_trunk(x_sample, mem_sample, norm_g, ev_w_in, ev_q_gain, ev_k_gain, ev_w_out,
                             od_w_in, od_conv_w, od_conv_b, od_a_log, od_dt_bias, od_d, od_norm_g, od_w_out,
                             xa_wq, xa_wkv, xa_wo, ffn_w_gu, ffn_w_down)
    return (y_prompt, y_sample)
```

```python
import functools

import jax
import jax.numpy as jnp
from jax import lax
from jax.experimental import pallas as pl
from jax.experimental.pallas import tpu as pltpu

F32 = jnp.float32
BF16 = jnp.bfloat16

EPS = 1e-6
ROPE_THETA = 10000.0
GRID_W = 64

RET_HEADS = 4
RET_DK = 256
RET_DV = 256
RET_DECAY_EXP_FWD = 5.0
RET_DECAY_EXP_BWD = 5.5
ATT_HEADS = 8
ATT_KV_HEADS = 2
ATT_HD = 128
ATT_GROUP = ATT_HEADS // ATT_KV_HEADS

SSD_HEADDIM = 64
SSD_STATE = 128
SSD_GROUPS = 8
SSD_CONV = 5
SSD_CHUNK = 128

XA_HEADS = 4

VMEM_LIMIT_BYTES = 56 * 1024 * 1024
LANES = 128
SUBLANES = 8

RET_CHUNK = 256
ATT_BLOCK_Q = 128
XA_BLOCK_Q = 512
CONV_ROWS = 256
NEG_BIG = -1e30


def _params(*sem):
    return pltpu.CompilerParams(dimension_semantics=sem, vmem_limit_bytes=VMEM_LIMIT_BYTES)


def _dot(a, b):
    return jnp.dot(a, b, preferred_element_type=F32)


def _dot_nt(a, b):
    return lax.dot_general(a, b, (((1,), (1,)), ((), ())), preferred_element_type=F32)


def _dot_tn(a, b):
    return lax.dot_general(a, b, (((0,), (0,)), ((), ())), preferred_element_type=F32)


def _silu(x):
    return x * jax.nn.sigmoid(x)


def _rms(x, g):
    return x * lax.rsqrt(jnp.mean(x * x, axis=-1, keepdims=True) + EPS) * g


def _pick_tile(n, pref):
    t = min(pref, n)
    while n % t:
        t //= 2
    return t


def _norm_matmul_kernel(x_ref, g_ref, w_ref, o_ref, h_ref):
    @pl.when(pl.program_id(1) == 0)
    def _():
        h_ref[...] = _rms(x_ref[...], g_ref[...]).astype(BF16)

    o_ref[...] = _dot(h_ref[...], w_ref[...]).astype(o_ref.dtype)


def norm_matmul(x, g, w, out_dtype=BF16, tm=512, tn=512):
    m, d = x.shape
    n = w.shape[1]
    tm = _pick_tile(m, tm)
    tn = _pick_tile(n, tn)
    return pl.pallas_call(
        _norm_matmul_kernel,
        grid=(m // tm, n // tn),
        in_specs=[
            pl.BlockSpec((tm, d), lambda i, j: (i, 0)),
            pl.BlockSpec((1, d), lambda i, j: (0, 0)),
            pl.BlockSpec((d, tn), lambda i, j: (0, j)),
        ],
        out_specs=pl.BlockSpec((tm, tn), lambda i, j: (i, j)),
        out_shape=jax.ShapeDtypeStruct((m, n), out_dtype),
        scratch_shapes=[pltpu.VMEM((tm, d), BF16)],
        compiler_params=_params("parallel", "arbitrary"),
        name="norm_matmul",
    )(x, g.reshape(1, d).astype(F32), w)


def _norm_swiglu_kernel(x_ref, g_ref, wg_ref, wu_ref, o_ref, h_ref):
    @pl.when(pl.program_id(1) == 0)
    def _():
        h_ref[...] = _rms(x_ref[...], g_ref[...]).astype(BF16)

    h = h_ref[...]
    gate = _dot(h, wg_ref[...])
    up = _dot(h, wu_ref[...])
    o_ref[...] = (_silu(gate) * up).astype(o_ref.dtype)


def norm_swiglu(x, g, w_gu, tm=512, tn=512):
    m, d = x.shape
    f = w_gu.shape[1] // 2
    tm = _pick_tile(m, tm)
    tn = _pick_tile(f, tn)
    nj = f // tn
    return pl.pallas_call(
        _norm_swiglu_kernel,
        grid=(m // tm, nj),
        in_specs=[
            pl.BlockSpec((tm, d), lambda i, j: (i, 0)),
            pl.BlockSpec((1, d), lambda i, j: (0, 0)),
            pl.BlockSpec((d, tn), lambda i, j: (0, j)),
            pl.BlockSpec((d, tn), lambda i, j: (0, j + nj)),
        ],
        out_specs=pl.BlockSpec((tm, tn), lambda i, j: (i, j)),
        out_shape=jax.ShapeDtypeStruct((m, f), BF16),
        scratch_shapes=[pltpu.VMEM((tm, d), BF16)],
        compiler_params=_params("parallel", "arbitrary"),
        name="norm_swiglu",
    )(x, g.reshape(1, d).astype(F32), w_gu, w_gu)


def _matmul_norm_resid_kernel(a_ref, w_ref, g_ref, x_ref, o_ref, acc_ref):
    k = pl.program_id(1)

    @pl.when(k == 0)
    def _():
        acc_ref[...] = jnp.zeros_like(acc_ref)

    acc_ref[...] += _dot(a_ref[...], w_ref[...])

    @pl.when(k == pl.num_programs(1) - 1)
    def _():
        o_ref[...] = x_ref[...] + _rms(acc_ref[...], g_ref[...])


def matmul_norm_resid(a, w, g, x, tm=512, tk=512):
    m, kdim = a.shape
    d = w.shape[1]
    tm = _pick_tile(m, tm)
    tk = _pick_tile(kdim, tk)
    return pl.pallas_call(
        _matmul_norm_resid_kernel,
        grid=(m // tm, kdim // tk),
        in_specs=[
            pl.BlockSpec((tm, tk), lambda i, k: (i, k)),
            pl.BlockSpec((tk, d), lambda i, k: (k, 0)),
            pl.BlockSpec((1, d), lambda i, k: (0, 0)),
            pl.BlockSpec((tm, d), lambda i, k: (i, 0)),
        ],
        out_specs=pl.BlockSpec((tm, d), lambda i, k: (i, 0)),
        out_shape=jax.ShapeDtypeStruct((m, d), F32),
        scratch_shapes=[pltpu.VMEM((tm, d), F32)],
        compiler_params=_params("parallel", "arbitrary"),
        name="matmul_norm_resid",
    )(a, w, g.reshape(1, d).astype(F32), x)


def _rope_angles(pos, dim):
    inv = ROPE_THETA ** (-jnp.arange(0, dim, 2, dtype=F32) / dim)
    ang = pos.astype(F32)[:, None] * inv[None, :]
    return jnp.cos(ang), jnp.sin(ang)


def _retention_tables(t):
    return _rope_angles(jnp.arange(t), RET_DK)


def _axial_tables(t):
    rows = t // GRID_W
    row = jnp.repeat(jnp.arange(rows), GRID_W)
    col = jnp.tile(jnp.arange(GRID_W), rows)
    half = ATT_HD // 2
    cos_r, sin_r = _rope_angles(row, half)
    cos_c, sin_c = _rope_angles(col, half)
    cos = jnp.concatenate([cos_r, cos_r, cos_c, cos_c], axis=-1)
    sin = jnp.concatenate([-sin_r, sin_r, -sin_c, sin_c], axis=-1)
    return cos, sin


def _retention_kernel(lg_ref, q_ref, k_ref, v_ref, g_ref, cos_ref, sin_ref, o_ref,
                      qs_ref, ks_ref, acc_ref, st_ref, *, chunk):
    t = q_ref.shape[1]
    dk = q_ref.shape[2]
    hk = dk // 2
    nc = t // chunk
    h = pl.program_id(1)
    lg_f = lg_ref[h, 0]
    lg_b = lg_ref[h, 1]

    def rope(x, cos, sin):
        x1 = x[:, :hk]
        x2 = x[:, hk:]
        return jnp.concatenate([x1 * cos - x2 * sin, x2 * cos + x1 * sin], axis=-1)

    def prep(c, carry):
        rows = pl.ds(pl.multiple_of(c * chunk, chunk), chunk)
        qs_ref[rows, :] = rope(q_ref[0, rows, :].astype(F32), cos_ref[rows, :], sin_ref[rows, :])
        ks_ref[rows, :] = rope(k_ref[0, rows, :].astype(F32), cos_ref[rows, :], sin_ref[rows, :]) * (dk ** -0.5)
        return carry

    lax.fori_loop(0, nc, prep, 0)

    idx = lax.broadcasted_iota(jnp.int32, (chunk, 1), 0).astype(F32)
    ri = lax.broadcasted_iota(jnp.int32, (chunk, chunk), 0)
    ci = lax.broadcasted_iota(jnp.int32, (chunk, chunk), 1)
    diff = (ri - ci).astype(F32)
    d_f = jnp.where(ri >= ci, jnp.exp(lg_f * jnp.where(ri >= ci, diff, 0.0)), 0.0)
    d_b = jnp.where(ri < ci, jnp.exp(lg_b * jnp.where(ri < ci, -diff, 0.0)), 0.0)
    xi_f = jnp.exp(lg_f * (idx + 1.0))
    zeta_f = jnp.exp(lg_f * (chunk - 1.0 - idx))
    xi_b = jnp.exp(lg_b * (chunk - idx))
    zeta_b = jnp.exp(lg_b * idx)
    one = jnp.ones((1, 1), F32)
    gc_f = jnp.exp(one * (lg_f * chunk))
    gc_b = jnp.exp(one * (lg_b * chunk))

    def scores(rows):
        qc = qs_ref[rows, :]
        kc = ks_ref[rows, :]
        kb = kc.astype(BF16)
        return qc, kb, _dot_nt(qc.astype(BF16), kb)

    def step(rows, dmat, xi, zeta, gc):
        qc, kb, s = scores(rows)
        vc = v_ref[0, rows, :]
        inner = _dot((s * dmat).astype(BF16), vc)
        cross = _dot((qc * xi).astype(BF16), st_ref[...].astype(BF16))
        st_ref[...] = gc * st_ref[...] + _dot_tn(kb, (vc.astype(F32) * zeta).astype(BF16))
        return inner + cross

    st_ref[...] = jnp.zeros_like(st_ref)

    def fwd(c, carry):
        rows = pl.ds(pl.multiple_of(c * chunk, chunk), chunk)
        acc_ref[rows, :] = step(rows, d_f, xi_f, zeta_f, gc_f)
        return carry

    lax.fori_loop(0, nc, fwd, 0)

    st_ref[...] = jnp.zeros_like(st_ref)

    def bwd(i, carry):
        c = nc - 1 - i
        rows = pl.ds(pl.multiple_of(c * chunk, chunk), chunk)
        ret = acc_ref[rows, :] + step(rows, d_b, xi_b, zeta_b, gc_b)
        mu = jnp.mean(ret, axis=-1, keepdims=True)
        cen = ret - mu
        var = jnp.mean(cen * cen, axis=-1, keepdims=True)
        gate = _silu(g_ref[0, rows, :].astype(F32))
        o_ref[0, rows, :] = (gate * (cen * lax.rsqrt(var + EPS))).astype(o_ref.dtype)
        return carry

    lax.fori_loop(0, nc, bwd, 0)


def retention(proj, lg, cos, sin):
    b, t, _ = proj.shape
    hh = RET_HEADS
    chunk = _pick_tile(t, RET_CHUNK)
    blk = lambda off: pl.BlockSpec((1, t, RET_DK), lambda bi, hi: (bi, 0, off + hi))
    return pl.pallas_call(
        functools.partial(_retention_kernel, chunk=chunk),
        grid=(b, hh),
        in_specs=[
            pl.BlockSpec(memory_space=pltpu.SMEM),
            blk(0), blk(hh), blk(2 * hh), blk(3 * hh),
            pl.BlockSpec((t, RET_DK // 2), lambda bi, hi: (0, 0)),
            pl.BlockSpec((t, RET_DK // 2), lambda bi, hi: (0, 0)),
        ],
        out_specs=pl.BlockSpec((1, t, RET_DV), lambda bi, hi: (bi, 0, hi)),
        out_shape=jax.ShapeDtypeStruct((b, t, hh * RET_DV), BF16),
        scratch_shapes=[
            pltpu.VMEM((t, RET_DK), F32),
            pltpu.VMEM((t, RET_DK), F32),
            pltpu.VMEM((t, RET_DV), F32),
            pltpu.VMEM((RET_DK, RET_DV), F32),
        ],
        compiler_params=_params("parallel", "parallel"),
        name="retention",
    )(lg, proj, proj, proj, proj, cos, sin)


def _axial_rope(x, cos, sin_signed):
    quarter = x.shape[-1] // 4
    lane = lax.broadcasted_iota(jnp.int32, x.shape, 1)
    first = (lane % (2 * quarter)) < quarter
    partner = jnp.where(first,
                        pltpu.roll(x, x.shape[-1] - quarter, axis=1),
                        pltpu.roll(x, quarter, axis=1))
    return x * cos + partner * sin_signed


def _gqa_kernel(q_ref, k_ref, v_ref, qg_ref, kg_ref, cosq_ref, sinq_ref, cosk_ref, sink_ref,
                o_ref, kt_ref):
    bq = q_ref.shape[1]
    hd = k_ref.shape[2]
    grp = q_ref.shape[2] // hd

    @pl.when(pl.program_id(2) == 0)
    def _():
        kn = _rms(k_ref[0].astype(F32), kg_ref[...])
        kt_ref[...] = _axial_rope(kn, cosk_ref[...], sink_ref[...]).astype(BF16)

    cos = cosq_ref[...]
    sin = sinq_ref[...]
    qs = []
    for h in range(grp):
        qn = _rms(q_ref[0, :, h * hd:(h + 1) * hd].astype(F32), qg_ref[...])
        qs.append((_axial_rope(qn, cos, sin) * (hd ** -0.5)).astype(BF16))
    q = jnp.concatenate(qs, axis=0)
    s = _dot_nt(q, kt_ref[...])
    m = jnp.max(s, axis=-1, keepdims=True)
    p = jnp.exp(s - m)
    l = jnp.sum(p, axis=-1, keepdims=True)
    o = _dot(p.astype(BF16), v_ref[0]) / l
    for h in range(grp):
        o_ref[0, :, h * hd:(h + 1) * hd] = o[h * bq:(h + 1) * bq].astype(o_ref.dtype)


def gqa_attention(proj, q_off, k_off, v_off, q_gain, k_gain, cos, sin):
    b, t, _ = proj.shape
    hd = ATT_HD
    gw = ATT_GROUP * hd
    bq = _pick_tile(t, ATT_BLOCK_Q)
    return pl.pallas_call(
        _gqa_kernel,
        grid=(b, ATT_KV_HEADS, t // bq),
        in_specs=[
            pl.BlockSpec((1, bq, gw), lambda bi, ki, qi: (bi, qi, q_off + ki)),
            pl.BlockSpec((1, t, hd), lambda bi, ki, qi: (bi, 0, k_off + ki)),
            pl.BlockSpec((1, t, hd), lambda bi, ki, qi: (bi, 0, v_off + ki)),
            pl.BlockSpec((1, hd), lambda bi, ki, qi: (0, 0)),
            pl.BlockSpec((1, hd), lambda bi, ki, qi: (0, 0)),
            pl.BlockSpec((bq, hd), lambda bi, ki, qi: (qi, 0)),
            pl.BlockSpec((bq, hd), lambda bi, ki, qi: (qi, 0)),
            pl.BlockSpec((t, hd), lambda bi, ki, qi: (0, 0)),
            pl.BlockSpec((t, hd), lambda bi, ki, qi: (0, 0)),
        ],
        out_specs=pl.BlockSpec((1, bq, gw), lambda bi, ki, qi: (bi, qi, ki)),
        out_shape=jax.ShapeDtypeStruct((b, t, ATT_HEADS * hd), BF16),
        scratch_shapes=[pltpu.VMEM((t, hd), BF16)],
        compiler_params=_params("parallel", "parallel", "arbitrary"),
        name="gqa_attention",
    )(proj, proj, proj, q_gain.reshape(1, hd).astype(F32), k_gain.reshape(1, hd).astype(F32),
      cos, sin, cos, sin)


def _cross_attn_kernel(q_ref, kv_ref, o_ref, *, heads):
    d = q_ref.shape[2]
    hd = d // heads
    for h in range(heads):
        qh = q_ref[0, :, h * hd:(h + 1) * hd]
        kh = kv_ref[0, :, h * hd:(h + 1) * hd]
        vh = kv_ref[0, :, d + h * hd:d + (h + 1) * hd]
        s = _dot_nt(qh, kh) * (hd ** -0.5)
        m = jnp.max(s, axis=-1, keepdims=True)
        p = jnp.exp(s - m)
        l = jnp.sum(p, axis=-1, keepdims=True)
        o_ref[0, :, h * hd:(h + 1) * hd] = (_dot(p.astype(BF16), vh) / l).astype(o_ref.dtype)


def cross_attention(q, kv):
    b, t, d = q.shape
    nm = kv.shape[1]
    tq = _pick_tile(t, XA_BLOCK_Q)
    return pl.pallas_call(
        functools.partial(_cross_attn_kernel, heads=XA_HEADS),
        grid=(b, t // tq),
        in_specs=[
            pl.BlockSpec((1, tq, d), lambda bi, qi: (bi, qi, 0)),
            pl.BlockSpec((1, nm, 2 * d), lambda bi, qi: (bi, 0, 0)),
        ],
        out_specs=pl.BlockSpec((1, tq, d), lambda bi, qi: (bi, qi, 0)),
        out_shape=jax.ShapeDtypeStruct((b, t, d), BF16),
        compiler_params=_params("parallel", "parallel"),
        name="cross_attention",
    )(q, kv)


def _conv_silu_kernel(x_ref, w_ref, b_ref, o_ref, pad_ref, *, rows):
    t = x_ref.shape[1]
    width = x_ref.shape[2]
    halo = SUBLANES
    taps = w_ref.shape[0]
    zeros = jnp.zeros((halo, width), F32)
    pad_ref[0:halo, :] = zeros
    pad_ref[halo + t:2 * halo + t, :] = zeros
    pad_ref[halo:halo + t, :] = x_ref[0].astype(F32)
    w = w_ref[...]
    bias = b_ref[...]

    def body(i, carry):
        r0 = pl.multiple_of(i * rows, rows)
        win = pad_ref[pl.ds(r0, rows + 2 * halo), :]
        acc = jnp.zeros((rows, width), F32)
        for k in range(taps):
            off = halo + k - taps // 2
            acc = acc + win[off:off + rows, :] * w[k:k + 1, :]
        o_ref[0, pl.ds(r0, rows), :] = _silu(acc + bias).astype(o_ref.dtype)
        return carry

    lax.fori_loop(0, t // rows, body, 0)


def conv_silu(zx, col_off, conv_w, conv_b, width=512):
    b, t, _ = zx.shape
    taps, c = conv_w.shape
    rows = _pick_tile(t, CONV_ROWS)
    return pl.pallas_call(
        functools.partial(_conv_silu_kernel, rows=rows),
        grid=(b, c // width),
        in_specs=[
            pl.BlockSpec((1, t, width), lambda bi, ci: (bi, 0, col_off + ci)),
            pl.BlockSpec((taps, width), lambda bi, ci: (0, ci)),
            pl.BlockSpec((1, width), lambda bi, ci: (0, ci)),
        ],
        out_specs=pl.BlockSpec((1, t, width), lambda bi, ci: (bi, 0, ci)),
        out_shape=jax.ShapeDtypeStruct((b, t, c), BF16),
        scratch_shapes=[pltpu.VMEM((t + 2 * SUBLANES, width), F32)],
        compiler_params=_params("parallel", "parallel"),
        name="conv_silu",
    )(zx, conv_w.astype(F32), conv_b.reshape(1, c).astype(F32))


def _dt_prep_kernel(raw_ref, bias_ref, alog_ref, dt_ref, acum_ref, *, chunk):
    t = raw_ref.shape[1]
    nh2 = raw_ref.shape[2]
    ri = lax.broadcasted_iota(jnp.int32, (chunk, chunk), 0)
    ci = lax.broadcasted_iota(jnp.int32, (chunk, chunk), 1)
    tri_f = (ci <= ri).astype(F32)
    tri_b = (ci >= ri).astype(F32)
    lane = lax.broadcasted_iota(jnp.int32, (chunk, nh2), 1)
    a = -jnp.exp(alog_ref[...])
    bias = bias_ref[...]

    def body(c, carry):
        rows = pl.ds(pl.multiple_of(c * chunk, chunk), chunk)
        dt = jax.nn.softplus(raw_ref[0, rows, :] + bias)
        dt_ref[0, rows, :] = dt
        da = dt * a
        pre = jnp.dot(tri_f, da, preferred_element_type=F32, precision=lax.Precision.HIGHEST)
        suf = jnp.dot(tri_b, da, preferred_element_type=F32, precision=lax.Precision.HIGHEST)
        acum_ref[0, rows, :] = jnp.where(lane < nh2 // 2, pre, suf)
        return carry

    lax.fori_loop(0, t // chunk, body, 0)


def dt_prep(dt_raw, dt_bias, a_log):
    b, t, nh2 = dt_raw.shape
    chunk = _pick_tile(t, SSD_CHUNK)
    spec = pl.BlockSpec((1, t, nh2), lambda bi: (bi, 0, 0))
    vec = pl.BlockSpec((1, nh2), lambda bi: (0, 0))
    return pl.pallas_call(
        functools.partial(_dt_prep_kernel, chunk=chunk),
        grid=(b,),
        in_specs=[spec, vec, vec],
        out_specs=[spec, spec],
        out_shape=[jax.ShapeDtypeStruct((b, t, nh2), F32)] * 2,
        compiler_params=_params("parallel"),
        name="ssd_dt_prep",
    )(dt_raw, dt_bias.reshape(1, nh2).astype(F32), a_log.reshape(1, nh2).astype(F32))


def _ssd_kernel(x_ref, b_ref, c_ref, z_ref, acol_ref, dcol_ref, arow_ref, dskip_ref, ng_ref,
                o_ref, yacc_ref, st_ref, *, chunk):
    t = x_ref.shape[1]
    width = x_ref.shape[2]
    npair = width // LANES
    hp = LANES // 2
    nc = t // chunk
    ri = lax.broadcasted_iota(jnp.int32, (chunk, chunk), 0)
    ci = lax.broadcasted_iota(jnp.int32, (chunk, chunk), 1)
    lane = lax.broadcasted_iota(jnp.int32, (chunk, LANES), 1)
    lo = lane < hp
    lane1 = lax.broadcasted_iota(jnp.int32, (1, LANES), 1)
    lo1 = lane1 < hp

    def pair(col, p):
        rows_n = col.shape[0]
        sel = lo if rows_n == chunk else lo1
        left = jnp.broadcast_to(col[:, 2 * p:2 * p + 1], (rows_n, LANES))
        right = jnp.broadcast_to(col[:, 2 * p + 1:2 * p + 2], (rows_n, LANES))
        return jnp.where(sel, left, right)

    def run(d):
        mask = (ri >= ci) if d == 0 else (ri <= ci)
        st_ref[...] = jnp.zeros_like(st_ref)

        def body(i, carry):
            c = i if d == 0 else nc - 1 - i
            rows = pl.ds(pl.multiple_of(c * chunk, chunk), chunk)
            x = x_ref[0, rows, :].astype(F32)
            bc = b_ref[0, rows, :]
            cc = c_ref[0, rows, :]
            acol = acol_ref[0, d, 0, rows, :]
            dcol = dcol_ref[0, d, 0, rows, :]
            arow = arow_ref[0, d, 0, c]
            a_end = acol[chunk - 1:chunk, :] if d == 0 else acol[0:1, :]
            ea = jnp.exp(acol)
            wcol = jnp.exp(a_end - acol) * dcol
            e_end = jnp.exp(a_end)

            cb = _dot_nt(cc, bc)
            coff = _dot(cc, st_ref[...].astype(BF16))

            ys, xws, ends = [], [], []
            for p in range(npair):
                xt = x[:, p * LANES:(p + 1) * LANES]
                xdt = xt * pair(dcol, p)
                y = coff[:, p * LANES:(p + 1) * LANES] * pair(ea, p)
                for half in range(2):
                    k = 2 * p + half
                    seg = acol[:, k:k + 1] - arow[k:k + 1, :]
                    mk = (cb * jnp.exp(jnp.where(mask, seg, NEG_BIG))).astype(BF16)
                    keep = lo if half == 0 else jnp.logical_not(lo)
                    y = y + _dot(mk, jnp.where(keep, xdt, 0.0).astype(BF16))
                ys.append(y)
                xws.append((xt * pair(wcol, p)).astype(BF16))
                ends.append(pair(e_end, p))
            y = jnp.concatenate(ys, axis=-1)
            xw = jnp.concatenate(xws, axis=-1)
            e_row = jnp.concatenate(ends, axis=-1)
            st_ref[...] = st_ref[...] * e_row + _dot_tn(bc, xw)

            if d == 0:
                yacc_ref[rows, :] = y
            else:
                tot = yacc_ref[rows, :] + y + dskip_ref[...] * x
                tot = tot * _silu(z_ref[0, rows, :].astype(F32))
                o_ref[0, rows, :] = _rms(tot, ng_ref[...]).astype(o_ref.dtype)
            return carry

        lax.fori_loop(0, nc, body, 0)

    run(0)
    run(1)


def ssd_scan(xbc, zx, acol, dcol, arow, d_skip_wide, norm_g):
    b, t, _ = xbc.shape
    g = SSD_GROUPS
    n = SSD_STATE
    inner = d_skip_wide.shape[0]
    width = inner // g
    kh = width // SSD_HEADDIM
    chunk = _pick_tile(t, SSD_CHUNK)
    b_off = inner // n
    c_off = b_off + g
    col_spec = pl.BlockSpec((1, 2, 1, t, kh), lambda bi, gi: (bi, 0, gi, 0, 0))
    return pl.pallas_call(
        functools.partial(_ssd_kernel, chunk=chunk),
        grid=(b, g),
        in_specs=[
            pl.BlockSpec((1, t, width), lambda bi, gi: (bi, 0, gi)),
            pl.BlockSpec((1, t, n), lambda bi, gi: (bi, 0, b_off + gi)),
            pl.BlockSpec((1, t, n), lambda bi, gi: (bi, 0, c_off + gi)),
            pl.BlockSpec((1, t, width), lambda bi, gi: (bi, 0, gi)),
            col_spec, col_spec,
            pl.BlockSpec((1, 2, 1, t // chunk, kh, chunk), lambda bi, gi: (bi, 0, gi, 0, 0, 0)),
            pl.BlockSpec((1, width), lambda bi, gi: (0, gi)),
            pl.BlockSpec((1, width), lambda bi, gi: (0, gi)),
        ],
        out_specs=pl.BlockSpec((1, t, width), lambda bi, gi: (bi, 0, gi)),
        out_shape=jax.ShapeDtypeStruct((b, t, inner), BF16),
        scratch_shapes=[pltpu.VMEM((t, width), F32), pltpu.VMEM((n, width), F32)],
        compiler_params=_params("parallel", "parallel"),
        name="ssd_scan",
    )(xbc, xbc, xbc, zx, acol, dcol, arow,
      d_skip_wide.reshape(1, inner).astype(F32), norm_g.reshape(1, inner).astype(F32))


def _even_mixer(xf, b, t, gn, w_in, q_gain, k_gain, w_out):
    d = xf.shape[1]
    proj = norm_matmul(xf, gn[0], w_in).reshape(b, t, -1)
    heads = jnp.arange(RET_HEADS, dtype=F32)
    lg = jnp.stack([jnp.log1p(-jnp.exp2(-(RET_DECAY_EXP_FWD + heads))),
                    jnp.log1p(-jnp.exp2(-(RET_DECAY_EXP_BWD + heads)))], axis=1)
    cos1, sin1 = _retention_tables(t)
    ret = retention(proj, lg, cos1, sin1)
    cos2, sin2 = _axial_tables(t)
    att_base = 2 * RET_HEADS * RET_DK + 2 * RET_HEADS * RET_DV
    gw = ATT_GROUP * ATT_HD
    k_base = att_base + ATT_HEADS * ATT_HD
    v_base = k_base + ATT_KV_HEADS * ATT_HD
    att = gqa_attention(proj, att_base // gw, k_base // ATT_HD, v_base // ATT_HD,
                        q_gain, k_gain, cos2, sin2)
    mix = jnp.concatenate([ret, att], axis=-1).reshape(b * t, -1)
    return matmul_norm_resid(mix, w_out, gn[1], xf)


def _odd_mixer(xf, b, t, gn, w_in_main, w_in_dt, conv_w, conv_b, a_log, dt_bias, d_skip, norm_g, w_out):
    inner = norm_g.shape[0]
    nh = d_skip.shape[0]
    g = SSD_GROUPS
    kh = nh // g
    zx = norm_matmul(xf, gn[0], w_in_main).reshape(b, t, -1)
    dt_raw = norm_matmul(xf, gn[0], w_in_dt, out_dtype=F32).reshape(b, t, 2 * nh)
    xbc = conv_silu(zx, inner // 512, conv_w, conv_b)
    dt, acum = dt_prep(dt_raw, dt_bias.reshape(-1), a_log.reshape(-1))
    grouped = lambda arr: arr.reshape(b, t, 2, g, kh)
    acol = grouped(acum).transpose(0, 2, 3, 1, 4)
    dcol = grouped(dt).transpose(0, 2, 3, 1, 4)
    chunk = _pick_tile(t, SSD_CHUNK)
    arow = acum.reshape(b, t // chunk, chunk, 2, g, kh).transpose(0, 3, 4, 1, 5, 2)
    y = ssd_scan(xbc, zx, acol, dcol, arow, jnp.repeat(d_skip.astype(F32), inner // nh), norm_g)
    return matmul_norm_resid(y.reshape(b * t, inner), w_out, gn[1], xf)


def _trunk(x, mem, norm_g, wts):
    b, t, d = x.shape
    nm = mem.shape[1]
    xf = x.reshape(b * t, d)
    memf = mem.reshape(b * nm, d)
    depth = norm_g.shape[0]
    for i in range(depth):
        gn = norm_g[i]
        j = i // 2
        if i % 2 == 0:
            xf = _even_mixer(xf, b, t, gn, wts["ev_w_in"][j], wts["ev_q_gain"][j], wts["ev_k_gain"][j],
                             wts["ev_w_out"][j])
        else:
            xf = _odd_mixer(xf, b, t, gn, wts["od_w_in_main"][j], wts["od_w_in_dt"][j], wts["od_conv_w"][j],
                            wts["od_conv_b"][j], wts["od_a_log"][j], wts["od_dt_bias"][j], wts["od_d"][j],
                            wts["od_norm_g"][j], wts["od_w_out"][j])
        q = norm_matmul(xf, gn[2], wts["xa_wq"][i]).reshape(b, t, d)
        kv = norm_matmul(memf, gn[4], wts["xa_wkv"][i]).reshape(b, nm, 2 * d)
        xa = cross_attention(q, kv).reshape(b * t, d)
        xf = matmul_norm_resid(xa, wts["xa_wo"][i], gn[3], xf)
        act = norm_swiglu(xf, gn[5], wts["ffn_w_gu"][i])
        xf = matmul_norm_resid(act, wts["ffn_w_down"][i], gn[6], xf)
    return xf.reshape(b, t, d)


def kernel(x_prompt, x_sample, mem_prompt, mem_sample, norm_g, ev_w_in, ev_q_gain, ev_k_gain, ev_w_out,
           od_w_in, od_conv_w, od_conv_b, od_a_log, od_dt_bias, od_d, od_norm_g, od_w_out,
           xa_wq, xa_wkv, xa_wo, ffn_w_gu, ffn_w_down):
    nh2 = od_a_log.shape[1] * od_a_log.shape[2]
    main = od_w_in.shape[2] - nh2
    wts = dict(
        ev_w_in=ev_w_in.astype(BF16), ev_q_gain=ev_q_gain, ev_k_gain=ev_k_gain,
        ev_w_out=ev_w_out.astype(BF16),
        od_w_in_main=od_w_in[:, :, :main].astype(BF16), od_w_in_dt=od_w_in[:, :, main:].astype(BF16),
        od_conv_w=od_conv_w, od_conv_b=od_conv_b, od_a_log=od_a_log, od_dt_bias=od_dt_bias,
        od_d=od_d, od_norm_g=od_norm_g, od_w_out=od_w_out.astype(BF16),
        xa_wq=xa_wq.astype(BF16), xa_wkv=xa_wkv.astype(BF16), xa_wo=xa_wo.astype(BF16),
        ffn_w_gu=ffn_w_gu.astype(BF16), ffn_w_down=ffn_w_down.astype(BF16),
    )
    y_prompt = _trunk(x_prompt, mem_prompt, norm_g, wts)
    y_sample = _trunk(x_sample, mem_sample, norm_g, wts)
    return (y_prompt, y_sample)
```

```python
import functools

import jax
import jax.numpy as jnp
from jax import lax
from jax.experimental import pallas as pl
from jax.experimental.pallas import tpu as pltpu

F32 = jnp.float32
BF16 = jnp.bfloat16

EPS = 1e-6
ROPE_THETA = 10000.0
GRID_W = 64

RET_HEADS = 4
RET_DK = 256
RET_DV = 256
RET_DECAY_EXP_FWD = 5.0
RET_DECAY_EXP_BWD = 5.5
ATT_HEADS = 8
ATT_KV_HEADS = 2
ATT_HD = 128
ATT_GROUP = ATT_HEADS // ATT_KV_HEADS

SSD_HEADDIM = 64
SSD_STATE = 128
SSD_GROUPS = 8
SSD_CONV = 5
SSD_CHUNK = 128

XA_HEADS = 4

VMEM_LIMIT_BYTES = 56 * 1024 * 1024
LANES = 128
SUBLANES = 8

RET_CHUNK = 256
ATT_BLOCK_Q = 128
XA_BLOCK_Q = 512
CONV_ROWS = 256
NEG_BIG = -1e30
RESIDENT_WEIGHT_BYTES = 24 * 1024 * 1024
N_CHUNK = 512
ATT_KV_BLOCKS = 4
LOG2E = 1.4426950408889634


def _params(*sem):
    return pltpu.CompilerParams(dimension_semantics=sem, vmem_limit_bytes=VMEM_LIMIT_BYTES)


def _dot(a, b):
    return jnp.dot(a, b, preferred_element_type=F32)


def _dot_nt(a, b):
    return lax.dot_general(a, b, (((1,), (1,)), ((), ())), preferred_element_type=F32)


def _dot_tn(a, b):
    return lax.dot_general(a, b, (((0,), (0,)), ((), ())), preferred_element_type=F32)


def _silu(x):
    return x * jax.nn.sigmoid(x)


def _rms(x, g):
    return x * lax.rsqrt(jnp.mean(x * x, axis=-1, keepdims=True) + EPS) * g


def _pick_tile(n, pref):
    t = min(pref, n)
    while n % t:
        t //= 2
    return t


def _resident(shape):
    zeros = (0,) * len(shape)
    return pl.BlockSpec(shape, lambda *_: zeros, pipeline_mode=pl.Buffered(1))


def _nbytes(a):
    return a.size * a.dtype.itemsize


def _norm_matmul_resident_kernel(x_ref, g_ref, w_ref, o_ref, h_ref, *, n_chunk):
    h_ref[...] = _rms(x_ref[...], g_ref[...]).astype(BF16)
    for c0 in range(0, w_ref.shape[1], n_chunk):
        cols = slice(c0, c0 + n_chunk)
        o_ref[:, cols] = _dot(h_ref[...], w_ref[:, cols]).astype(o_ref.dtype)


def _norm_matmul_kernel(x_ref, g_ref, w_ref, *rest, has_extra):
    if has_extra:
        w2_ref, o_ref, o2_ref, h_ref = rest
    else:
        o_ref, h_ref = rest

    @pl.when(pl.program_id(1) == 0)
    def _():
        h_ref[...] = _rms(x_ref[...], g_ref[...]).astype(BF16)
        if has_extra:
            o2_ref[...] = _dot(h_ref[...], w2_ref[...])

    o_ref[...] = _dot(h_ref[...], w_ref[...]).astype(o_ref.dtype)


def norm_matmul(x, g, w, w_extra=None):
    m, d = x.shape
    n = w.shape[1]
    g2 = g.reshape(1, d).astype(F32)
    if w_extra is None and _nbytes(w) <= RESIDENT_WEIGHT_BYTES:
        tm = _pick_tile(m, 1024 if _nbytes(w) <= RESIDENT_WEIGHT_BYTES // 3 else 512)
        return pl.pallas_call(
            functools.partial(_norm_matmul_resident_kernel, n_chunk=_pick_tile(n, N_CHUNK)),
            grid=(m // tm,),
            in_specs=[pl.BlockSpec((tm, d), lambda i: (i, 0)), _resident((1, d)), _resident((d, n))],
            out_specs=pl.BlockSpec((tm, n), lambda i: (i, 0)),
            out_shape=jax.ShapeDtypeStruct((m, n), BF16),
            scratch_shapes=[pltpu.VMEM((tm, d), BF16)],
            compiler_params=_params("parallel"),
            name="norm_matmul_resident",
        )(x, g2, w)
    tm = _pick_tile(m, 1024)
    tn = _pick_tile(n, 1024)
    in_specs = [
        pl.BlockSpec((tm, d), lambda i, j: (i, 0)),
        _resident((1, d)),
        pl.BlockSpec((d, tn), lambda i, j: (0, j)),
    ]
    out_specs = pl.BlockSpec((tm, tn), lambda i, j: (i, j))
    out_shape = jax.ShapeDtypeStruct((m, n), BF16)
    args = (x, g2, w)
    if w_extra is not None:
        n2 = w_extra.shape[1]
        in_specs.append(_resident((d, n2)))
        out_specs = [out_specs, pl.BlockSpec((tm, n2), lambda i, j: (i, 0))]
        out_shape = [out_shape, jax.ShapeDtypeStruct((m, n2), F32)]
        args = args + (w_extra,)
    return pl.pallas_call(
        functools.partial(_norm_matmul_kernel, has_extra=w_extra is not None),
        grid=(m // tm, n // tn),
        in_specs=in_specs,
        out_specs=out_specs,
        out_shape=out_shape,
        scratch_shapes=[pltpu.VMEM((tm, d), BF16)],
        compiler_params=_params("parallel", "arbitrary"),
        name="norm_matmul",
    )(*args)


def _norm_swiglu_kernel(x_ref, g_ref, wg_ref, wu_ref, o_ref, h_ref):
    @pl.when(pl.program_id(1) == 0)
    def _():
        h_ref[...] = _rms(x_ref[...], g_ref[...]).astype(BF16)

    h = h_ref[...]
    gate = _dot(h, wg_ref[...])
    up = _dot(h, wu_ref[...])
    o_ref[...] = (_silu(gate) * up).astype(o_ref.dtype)


def norm_swiglu(x, g, w_gu, tm=1024, tn=512):
    m, d = x.shape
    f = w_gu.shape[1] // 2
    tm = _pick_tile(m, tm)
    tn = _pick_tile(f, tn)
    nj = f // tn
    return pl.pallas_call(
        _norm_swiglu_kernel,
        grid=(m // tm, nj),
        in_specs=[
            pl.BlockSpec((tm, d), lambda i, j: (i, 0)),
            pl.BlockSpec((1, d), lambda i, j: (0, 0)),
            pl.BlockSpec((d, tn), lambda i, j: (0, j)),
            pl.BlockSpec((d, tn), lambda i, j: (0, j + nj)),
        ],
        out_specs=pl.BlockSpec((tm, tn), lambda i, j: (i, j)),
        out_shape=jax.ShapeDtypeStruct((m, f), BF16),
        scratch_shapes=[pltpu.VMEM((tm, d), BF16)],
        compiler_params=_params("parallel", "arbitrary"),
        name="norm_swiglu",
    )(x, g.reshape(1, d).astype(F32), w_gu, w_gu)


def _matmul_norm_resid_kernel(*refs, n_a, n_chunk):
    a_refs = refs[:n_a]
    w_refs = refs[n_a:2 * n_a]
    g_ref, x_ref, o_ref = refs[2 * n_a:]
    tm, d = o_ref.shape
    ss = jnp.zeros((tm, 1), F32)
    for c0 in range(0, d, n_chunk):
        cols = slice(c0, c0 + n_chunk)
        y = _dot(a_refs[0][...], w_refs[0][:, cols])
        for a_ref, w_ref in zip(a_refs[1:], w_refs[1:]):
            y = y + _dot(a_ref[...], w_ref[:, cols])
        o_ref[:, cols] = y
        ss = ss + jnp.sum(y * y, axis=-1, keepdims=True)
    scale = lax.rsqrt(ss / d + EPS)
    for c0 in range(0, d, n_chunk):
        cols = slice(c0, c0 + n_chunk)
        o_ref[:, cols] = x_ref[:, cols] + o_ref[:, cols] * scale * g_ref[:, cols]


def matmul_norm_resid(parts, w, g, x, tm=512):
    m = parts[0].shape[0]
    d = w.shape[1]
    tm = _pick_tile(m, tm)
    n_a = len(parts)
    kp = parts[0].shape[1]
    assert all(p.shape[1] == kp for p in parts) and n_a * kp == w.shape[0]
    assert _nbytes(w) <= RESIDENT_WEIGHT_BYTES
    in_specs = [pl.BlockSpec((tm, kp), lambda i: (i, 0)) for _ in parts]
    in_specs += [pl.BlockSpec((kp, d), lambda i, r=r: (r, 0), pipeline_mode=pl.Buffered(1)) for r in range(n_a)]
    in_specs += [_resident((1, d)), pl.BlockSpec((tm, d), lambda i: (i, 0))]
    return pl.pallas_call(
        functools.partial(_matmul_norm_resid_kernel, n_a=n_a, n_chunk=_pick_tile(d, N_CHUNK)),
        grid=(m // tm,),
        in_specs=in_specs,
        out_specs=pl.BlockSpec((tm, d), lambda i: (i, 0)),
        out_shape=jax.ShapeDtypeStruct((m, d), F32),
        compiler_params=_params("parallel"),
        name="matmul_norm_resid",
    )(*parts, *([w] * n_a), g.reshape(1, d).astype(F32), x)


def _rope_angles(pos, dim):
    inv = ROPE_THETA ** (-jnp.arange(0, dim, 2, dtype=F32) / dim)
    ang = pos.astype(F32)[:, None] * inv[None, :]
    return jnp.cos(ang), jnp.sin(ang)


def _retention_tables(t):
    return _rope_angles(jnp.arange(t), RET_DK)


def _axial_tables(t):
    rows = t // GRID_W
    row = jnp.repeat(jnp.arange(rows), GRID_W)
    col = jnp.tile(jnp.arange(GRID_W), rows)
    half = ATT_HD // 2
    cos_r, sin_r = _rope_angles(row, half)
    cos_c, sin_c = _rope_angles(col, half)
    cos = jnp.concatenate([cos_r, cos_r, cos_c, cos_c], axis=-1)
    sin = jnp.concatenate([-sin_r, sin_r, -sin_c, sin_c], axis=-1)
    return cos, sin


def _retention_kernel(lg_ref, q_ref, k_ref, v_ref, g_ref, cos_ref, sin_ref, o_ref,
                      qs_ref, ks_ref, acc_ref, st_ref, *, chunk):
    t = q_ref.shape[1]
    dk = q_ref.shape[2]
    hk = dk // 2
    nc = t // chunk
    h = pl.program_id(1)
    lg_f = lg_ref[h, 0]
    lg_b = lg_ref[h, 1]

    def rope(x, cos, sin):
        x1 = x[:, :hk]
        x2 = x[:, hk:]
        return jnp.concatenate([x1 * cos - x2 * sin, x2 * cos + x1 * sin], axis=-1)

    def prep(c, carry):
        rows = pl.ds(pl.multiple_of(c * chunk, chunk), chunk)
        qs_ref[rows, :] = rope(q_ref[0, rows, :].astype(F32), cos_ref[rows, :], sin_ref[rows, :])
        ks_ref[rows, :] = rope(k_ref[0, rows, :].astype(F32), cos_ref[rows, :], sin_ref[rows, :]) * (dk ** -0.5)
        return carry

    lax.fori_loop(0, nc, prep, 0)

    idx = lax.broadcasted_iota(jnp.int32, (chunk, 1), 0).astype(F32)
    ri = lax.broadcasted_iota(jnp.int32, (chunk, chunk), 0)
    ci = lax.broadcasted_iota(jnp.int32, (chunk, chunk), 1)
    diff = (ri - ci).astype(F32)
    d_f = jnp.where(ri >= ci, jnp.exp(lg_f * jnp.where(ri >= ci, diff, 0.0)), 0.0)
    d_b = jnp.where(ri < ci, jnp.exp(lg_b * jnp.where(ri < ci, -diff, 0.0)), 0.0)
    xi_f = jnp.exp(lg_f * (idx + 1.0))
    zeta_f = jnp.exp(lg_f * (chunk - 1.0 - idx))
    xi_b = jnp.exp(lg_b * (chunk - idx))
    zeta_b = jnp.exp(lg_b * idx)
    one = jnp.ones((1, 1), F32)
    gc_f = jnp.exp(one * (lg_f * chunk))
    gc_b = jnp.exp(one * (lg_b * chunk))

    def scores(rows):
        qc = qs_ref[rows, :]
        kc = ks_ref[rows, :]
        kb = kc.astype(BF16)
        return qc, kb, _dot_nt(qc.astype(BF16), kb)

    def step(rows, dmat, xi, zeta, gc):
        qc, kb, s = scores(rows)
        vc = v_ref[0, rows, :]
        inner = _dot((s * dmat).astype(BF16), vc)
        cross = _dot((qc * xi).astype(BF16), st_ref[...].astype(BF16))
        st_ref[...] = gc * st_ref[...] + _dot_tn(kb, (vc.astype(F32) * zeta).astype(BF16))
        return inner + cross

    st_ref[...] = jnp.zeros_like(st_ref)

    def fwd(c, carry):
        rows = pl.ds(pl.multiple_of(c * chunk, chunk), chunk)
        acc_ref[rows, :] = step(rows, d_f, xi_f, zeta_f, gc_f)
        return carry

    lax.fori_loop(0, nc, fwd, 0)

    st_ref[...] = jnp.zeros_like(st_ref)

    def bwd(i, carry):
        c = nc - 1 - i
        rows = pl.ds(pl.multiple_of(c * chunk, chunk), chunk)
        ret = acc_ref[rows, :] + step(rows, d_b, xi_b, zeta_b, gc_b)
        mu = jnp.mean(ret, axis=-1, keepdims=True)
        cen = ret - mu
        var = jnp.mean(cen * cen, axis=-1, keepdims=True)
        gate = _silu(g_ref[0, rows, :].astype(F32))
        o_ref[0, rows, :] = (gate * (cen * lax.rsqrt(var + EPS))).astype(o_ref.dtype)
        return carry

    lax.fori_loop(0, nc, bwd, 0)


def retention(proj, lg, cos, sin):
    b, t, _ = proj.shape
    hh = RET_HEADS
    chunk = _pick_tile(t, RET_CHUNK)
    blk = lambda off: pl.BlockSpec((1, t, RET_DK), lambda bi, hi: (bi, 0, off + hi))
    return pl.pallas_call(
        functools.partial(_retention_kernel, chunk=chunk),
        grid=(b, hh),
        in_specs=[
            pl.BlockSpec(memory_space=pltpu.SMEM),
            blk(0), blk(hh), blk(2 * hh), blk(3 * hh),
            pl.BlockSpec((t, RET_DK // 2), lambda bi, hi: (0, 0)),
            pl.BlockSpec((t, RET_DK // 2), lambda bi, hi: (0, 0)),
        ],
        out_specs=pl.BlockSpec((1, t, RET_DV), lambda bi, hi: (bi, 0, hi)),
        out_shape=jax.ShapeDtypeStruct((b, t, hh * RET_DV), BF16),
        scratch_shapes=[
            pltpu.VMEM((t, RET_DK), F32),
            pltpu.VMEM((t, RET_DK), F32),
            pltpu.VMEM((t, RET_DV), F32),
            pltpu.VMEM((RET_DK, RET_DV), F32),
        ],
        compiler_params=_params("parallel", "parallel"),
        name="retention",
    )(lg, proj, proj, proj, proj, cos, sin)


def _axial_rope(x, cos, sin_signed):
    quarter = x.shape[-1] // 4
    lane = lax.broadcasted_iota(jnp.int32, x.shape, 1)
    first = (lane % (2 * quarter)) < quarter
    partner = jnp.where(first,
                        pltpu.roll(x, x.shape[-1] - quarter, axis=1),
                        pltpu.roll(x, quarter, axis=1))
    return x * cos + partner * sin_signed


def _gqa_kernel(q_ref, k_ref, v_ref, qg_ref, kg_ref, cosq_ref, sinq_ref, cosk_ref, sink_ref,
                o_ref, kt_ref):
    bq = q_ref.shape[1]
    hd = k_ref.shape[2]
    grp = q_ref.shape[2] // hd

    @pl.when(pl.program_id(2) == 0)
    def _():
        kn = _rms(k_ref[0].astype(F32), kg_ref[...])
        kt_ref[...] = _axial_rope(kn, cosk_ref[...], sink_ref[...]).astype(BF16)

    cos = cosq_ref[...]
    sin = sinq_ref[...]
    qs = []
    for h in range(grp):
        qn = _rms(q_ref[0, :, h * hd:(h + 1) * hd].astype(F32), qg_ref[...])
        qs.append((_axial_rope(qn, cos, sin) * (hd ** -0.5 * LOG2E)).astype(BF16))
    q = jnp.concatenate(qs, axis=0)
    t = kt_ref.shape[0]
    kb = t // _pick_tile(t // LANES, ATT_KV_BLOCKS)
    blocks = [slice(j, j + kb) for j in range(0, t, kb)]
    m = l = acc = None
    for blk in blocks:
        s = _dot_nt(q, kt_ref[blk, :])
        m_blk = jnp.max(s, axis=-1, keepdims=True)
        if m is None:
            m = m_blk
            p = jnp.exp2(s - m)
            l = jnp.sum(p, axis=-1, keepdims=True)
            acc = _dot(p.astype(BF16), v_ref[0, blk, :])
        else:
            m_new = jnp.maximum(m, m_blk)
            alpha = jnp.exp2(m - m_new)
            p = jnp.exp2(s - m_new)
            l = l * alpha + jnp.sum(p, axis=-1, keepdims=True)
            acc = acc * alpha + _dot(p.astype(BF16), v_ref[0, blk, :])
            m = m_new
    o = acc / l
    for h in range(grp):
        o_ref[0, :, h * hd:(h + 1) * hd] = o[h * bq:(h + 1) * bq].astype(o_ref.dtype)


def gqa_attention(proj, q_off, k_off, v_off, q_gain, k_gain, cos, sin):
    b, t, _ = proj.shape
    hd = ATT_HD
    gw = ATT_GROUP * hd
    bq = _pick_tile(t, ATT_BLOCK_Q)
    return pl.pallas_call(
        _gqa_kernel,
        grid=(b, ATT_KV_HEADS, t // bq),
        in_specs=[
            pl.BlockSpec((1, bq, gw), lambda bi, ki, qi: (bi, qi, q_off + ki)),
            pl.BlockSpec((1, t, hd), lambda bi, ki, qi: (bi, 0, k_off + ki)),
            pl.BlockSpec((1, t, hd), lambda bi, ki, qi: (bi, 0, v_off + ki)),
            pl.BlockSpec((1, hd), lambda bi, ki, qi: (0, 0)),
            pl.BlockSpec((1, hd), lambda bi, ki, qi: (0, 0)),
            pl.BlockSpec((bq, hd), lambda bi, ki, qi: (qi, 0)),
            pl.BlockSpec((bq, hd), lambda bi, ki, qi: (qi, 0)),
            pl.BlockSpec((t, hd), lambda bi, ki, qi: (0, 0)),
            pl.BlockSpec((t, hd), lambda bi, ki, qi: (0, 0)),
        ],
        out_specs=pl.BlockSpec((1, bq, gw), lambda bi, ki, qi: (bi, qi, ki)),
        out_shape=jax.ShapeDtypeStruct((b, t, ATT_HEADS * hd), BF16),
        scratch_shapes=[pltpu.VMEM((t, hd), BF16)],
        compiler_params=_params("parallel", "parallel", "arbitrary"),
        name="gqa_attention",
    )(proj, proj, proj, q_gain.reshape(1, hd).astype(F32), k_gain.reshape(1, hd).astype(F32),
      cos, sin, cos, sin)


def _cross_attn_kernel(q_ref, kv_ref, o_ref, *, heads):
    d = q_ref.shape[2]
    hd = d // heads
    for h in range(heads):
        qh = q_ref[0, :, h * hd:(h + 1) * hd]
        kh = kv_ref[0, :, h * hd:(h + 1) * hd]
        vh = kv_ref[0, :, d + h * hd:d + (h + 1) * hd]
        s = _dot_nt(qh, kh) * (hd ** -0.5)
        m = jnp.max(s, axis=-1, keepdims=True)
        p = jnp.exp(s - m)
        l = jnp.sum(p, axis=-1, keepdims=True)
        o_ref[0, :, h * hd:(h + 1) * hd] = (_dot(p.astype(BF16), vh) / l).astype(o_ref.dtype)


def cross_attention(q, kv):
    b, t, d = q.shape
    nm = kv.shape[1]
    tq = _pick_tile(t, XA_BLOCK_Q)
    return pl.pallas_call(
        functools.partial(_cross_attn_kernel, heads=XA_HEADS),
        grid=(b, t // tq),
        in_specs=[
            pl.BlockSpec((1, tq, d), lambda bi, qi: (bi, qi, 0)),
            pl.BlockSpec((1, nm, 2 * d), lambda bi, qi: (bi, 0, 0)),
        ],
        out_specs=pl.BlockSpec((1, tq, d), lambda bi, qi: (bi, qi, 0)),
        out_shape=jax.ShapeDtypeStruct((b, t, d), BF16),
        compiler_params=_params("parallel", "parallel"),
        name="cross_attention",
    )(q, kv)


def _conv_silu_kernel(x_ref, w_ref, b_ref, o_ref, pad_ref, *, rows):
    t = x_ref.shape[1]
    width = x_ref.shape[2]
    halo = SUBLANES
    taps = w_ref.shape[0]
    zeros = jnp.zeros((halo, width), F32)
    pad_ref[0:halo, :] = zeros
    pad_ref[halo + t:2 * halo + t, :] = zeros
    pad_ref[halo:halo + t, :] = x_ref[0].astype(F32)
    w = w_ref[...]
    bias = b_ref[...]

    def body(i, carry):
        r0 = pl.multiple_of(i * rows, rows)
        win = pad_ref[pl.ds(r0, rows + 2 * halo), :]
        acc = jnp.zeros((rows, width), F32)
        for k in range(taps):
            off = halo + k - taps // 2
            acc = acc + win[off:off + rows, :] * w[k:k + 1, :]
        o_ref[0, pl.ds(r0, rows), :] = _silu(acc + bias).astype(o_ref.dtype)
        return carry

    lax.fori_loop(0, t // rows, body, 0)


def conv_silu(zx, col_off, conv_w, conv_b, width=512):
    b, t, _ = zx.shape
    taps, c = conv_w.shape
    rows = _pick_tile(t, CONV_ROWS)
    return pl.pallas_call(
        functools.partial(_conv_silu_kernel, rows=rows),
        grid=(b, c // width),
        in_specs=[
            pl.BlockSpec((1, t, width), lambda bi, ci: (bi, 0, col_off + ci)),
            pl.BlockSpec((taps, width), lambda bi, ci: (0, ci)),
            pl.BlockSpec((1, width), lambda bi, ci: (0, ci)),
        ],
        out_specs=pl.BlockSpec((1, t, width), lambda bi, ci: (bi, 0, ci)),
        out_shape=jax.ShapeDtypeStruct((b, t, c), BF16),
        scratch_shapes=[pltpu.VMEM((t + 2 * SUBLANES, width), F32)],
        compiler_params=_params("parallel", "parallel"),
        name="conv_silu",
    )(zx, conv_w.astype(F32), conv_b.reshape(1, c).astype(F32))


def _dt_prep_kernel(raw_ref, bias_ref, alog_ref, dt_ref, acum_ref, *, chunk):
    t = raw_ref.shape[1]
    nh2 = raw_ref.shape[2]
    ri = lax.broadcasted_iota(jnp.int32, (chunk, chunk), 0)
    ci = lax.broadcasted_iota(jnp.int32, (chunk, chunk), 1)
    tri_f = (ci <= ri).astype(F32)
    tri_b = (ci >= ri).astype(F32)
    lane = lax.broadcasted_iota(jnp.int32, (chunk, nh2), 1)
    a = -jnp.exp(alog_ref[...])
    bias = bias_ref[...]

    def body(c, carry):
        rows = pl.ds(pl.multiple_of(c * chunk, chunk), chunk)
        dt = jax.nn.softplus(raw_ref[0, rows, :] + bias)
        dt_ref[0, rows, :] = dt
        da = dt * a
        pre = jnp.dot(tri_f, da, preferred_element_type=F32, precision=lax.Precision.HIGHEST)
        suf = jnp.dot(tri_b, da, preferred_element_type=F32, precision=lax.Precision.HIGHEST)
        acum_ref[0, rows, :] = jnp.where(lane < nh2 // 2, pre, suf)
        return carry

    lax.fori_loop(0, t // chunk, body, 0)


def dt_prep(dt_raw, dt_bias, a_log):
    b, t, nh2 = dt_raw.shape
    chunk = _pick_tile(t, SSD_CHUNK)
    spec = pl.BlockSpec((1, t, nh2), lambda bi: (bi, 0, 0))
    vec = pl.BlockSpec((1, nh2), lambda bi: (0, 0))
    return pl.pallas_call(
        functools.partial(_dt_prep_kernel, chunk=chunk),
        grid=(b,),
        in_specs=[spec, vec, vec],
        out_specs=[spec, spec],
        out_shape=[jax.ShapeDtypeStruct((b, t, nh2), F32)] * 2,
        compiler_params=_params("parallel"),
        name="ssd_dt_prep",
    )(dt_raw, dt_bias.reshape(1, nh2).astype(F32), a_log.reshape(1, nh2).astype(F32))


def _ssd_kernel(x_ref, b_ref, c_ref, z_ref, acol_ref, dcol_ref, arow_ref, dskip_ref, ng_ref,
                o_ref, yacc_ref, st_ref, *, chunk):
    t = x_ref.shape[1]
    width = x_ref.shape[2]
    npair = width // LANES
    hp = LANES // 2
    nc = t // chunk
    ri = lax.broadcasted_iota(jnp.int32, (chunk, chunk), 0)
    ci = lax.broadcasted_iota(jnp.int32, (chunk, chunk), 1)
    lane = lax.broadcasted_iota(jnp.int32, (chunk, LANES), 1)
    lo = lane < hp
    lane1 = lax.broadcasted_iota(jnp.int32, (1, LANES), 1)
    lo1 = lane1 < hp

    def pair(col, p):
        rows_n = col.shape[0]
        sel = lo if rows_n == chunk else lo1
        left = jnp.broadcast_to(col[:, 2 * p:2 * p + 1], (rows_n, LANES))
        right = jnp.broadcast_to(col[:, 2 * p + 1:2 * p + 2], (rows_n, LANES))
        return jnp.where(sel, left, right)

    def run(d):
        mask = (ri >= ci) if d == 0 else (ri <= ci)
        st_ref[...] = jnp.zeros_like(st_ref)

        def body(i, carry):
            c = i if d == 0 else nc - 1 - i
            rows = pl.ds(pl.multiple_of(c * chunk, chunk), chunk)
            x = x_ref[0, rows, :].astype(F32)
            bc = b_ref[0, rows, :]
            cc = c_ref[0, rows, :]
            acol = acol_ref[0, d, 0, rows, :]
            dcol = dcol_ref[0, d, 0, rows, :]
            arow = arow_ref[0, d, 0, c]
            a_end = acol[chunk - 1:chunk, :] if d == 0 else acol[0:1, :]
            ea = jnp.exp(acol)
            wcol = jnp.exp(a_end - acol) * dcol
            e_end = jnp.exp(a_end)

            cb = _dot_nt(cc, bc)
            coff = _dot(cc, st_ref[...].astype(BF16))

            ys, xws, ends = [], [], []
            for p in range(npair):
                xt = x[:, p * LANES:(p + 1) * LANES]
                xdt = xt * pair(dcol, p)
                y = coff[:, p * LANES:(p + 1) * LANES] * pair(ea, p)
                for half in range(2):
                    k = 2 * p + half
                    seg = acol[:, k:k + 1] - arow[k:k + 1, :]
                    mk = (cb * jnp.exp(jnp.where(mask, seg, NEG_BIG))).astype(BF16)
                    keep = lo if half == 0 else jnp.logical_not(lo)
                    y = y + _dot(mk, jnp.where(keep, xdt, 0.0).astype(BF16))
                ys.append(y)
                xws.append((xt * pair(wcol, p)).astype(BF16))
                ends.append(pair(e_end, p))
            y = jnp.concatenate(ys, axis=-1)
            xw = jnp.concatenate(xws, axis=-1)
            e_row = jnp.concatenate(ends, axis=-1)
            st_ref[...] = st_ref[...] * e_row + _dot_tn(bc, xw)

            if d == 0:
                yacc_ref[rows, :] = y
            else:
                tot = yacc_ref[rows, :] + y + dskip_ref[...] * x
                tot = tot * _silu(z_ref[0, rows, :].astype(F32))
                o_ref[0, rows, :] = _rms(tot, ng_ref[...]).astype(o_ref.dtype)
            return carry

        lax.fori_loop(0, nc, body, 0)

    run(0)
    run(1)


def ssd_scan(xbc, zx, acol, dcol, arow, d_skip_wide, norm_g):
    b, t, _ = xbc.shape
    g = SSD_GROUPS
    n = SSD_STATE
    inner = d_skip_wide.shape[0]
    width = inner // g
    kh = width // SSD_HEADDIM
    chunk = _pick_tile(t, SSD_CHUNK)
    b_off = inner // n
    c_off = b_off + g
    col_spec = pl.BlockSpec((1, 2, 1, t, kh), lambda bi, gi: (bi, 0, gi, 0, 0))
    return pl.pallas_call(
        functools.partial(_ssd_kernel, chunk=chunk),
        grid=(b, g),
        in_specs=[
            pl.BlockSpec((1, t, width), lambda bi, gi: (bi, 0, gi)),
            pl.BlockSpec((1, t, n), lambda bi, gi: (bi, 0, b_off + gi)),
            pl.BlockSpec((1, t, n), lambda bi, gi: (bi, 0, c_off + gi)),
            pl.BlockSpec((1, t, width), lambda bi, gi: (bi, 0, gi)),
            col_spec, col_spec,
            pl.BlockSpec((1, 2, 1, t // chunk, kh, chunk), lambda bi, gi: (bi, 0, gi, 0, 0, 0)),
            pl.BlockSpec((1, width), lambda bi, gi: (0, gi)),
            pl.BlockSpec((1, width), lambda bi, gi: (0, gi)),
        ],
        out_specs=pl.BlockSpec((1, t, width), lambda bi, gi: (bi, 0, gi)),
        out_shape=jax.ShapeDtypeStruct((b, t, inner), BF16),
        scratch_shapes=[pltpu.VMEM((t, width), F32), pltpu.VMEM((n, width), F32)],
        compiler_params=_params("parallel", "parallel"),
        name="ssd_scan",
    )(xbc, xbc, xbc, zx, acol, dcol, arow,
      d_skip_wide.reshape(1, inner).astype(F32), norm_g.reshape(1, inner).astype(F32))


def _even_mixer(xf, b, t, gn, w_in, q_gain, k_gain, w_out):
    d = xf.shape[1]
    proj = norm_matmul(xf, gn[0], w_in).reshape(b, t, -1)
    heads = jnp.arange(RET_HEADS, dtype=F32)
    lg = jnp.stack([jnp.log1p(-jnp.exp2(-(RET_DECAY_EXP_FWD + heads))),
                    jnp.log1p(-jnp.exp2(-(RET_DECAY_EXP_BWD + heads)))], axis=1)
    cos1, sin1 = _retention_tables(t)
    ret = retention(proj, lg, cos1, sin1)
    cos2, sin2 = _axial_tables(t)
    att_base = 2 * RET_HEADS * RET_DK + 2 * RET_HEADS * RET_DV
    gw = ATT_GROUP * ATT_HD
    k_base = att_base + ATT_HEADS * ATT_HD
    v_base = k_base + ATT_KV_HEADS * ATT_HD
    att = gqa_attention(proj, att_base // gw, k_base // ATT_HD, v_base // ATT_HD,
                        q_gain, k_gain, cos2, sin2)
    parts = [ret.reshape(b * t, -1), att.reshape(b * t, -1)]
    return matmul_norm_resid(parts, w_out, gn[1], xf)


def _odd_mixer(xf, b, t, gn, w_in_main, w_in_dt, conv_w, conv_b, a_log, dt_bias, d_skip, norm_g, w_out):
    inner = norm_g.shape[0]
    nh = d_skip.shape[0]
    g = SSD_GROUPS
    kh = nh // g
    zx, dt_raw = norm_matmul(xf, gn[0], w_in_main, w_extra=w_in_dt)
    zx = zx.reshape(b, t, -1)
    dt_raw = dt_raw.reshape(b, t, 2 * nh)
    xbc = conv_silu(zx, inner // 512, conv_w, conv_b)
    dt, acum = dt_prep(dt_raw, dt_bias.reshape(-1), a_log.reshape(-1))
    grouped = lambda arr: arr.reshape(b, t, 2, g, kh)
    acol = grouped(acum).transpose(0, 2, 3, 1, 4)
    dcol = grouped(dt).transpose(0, 2, 3, 1, 4)
    chunk = _pick_tile(t, SSD_CHUNK)
    arow = acum.reshape(b, t // chunk, chunk, 2, g, kh).transpose(0, 3, 4, 1, 5, 2)
    y = ssd_scan(xbc, zx, acol, dcol, arow, jnp.repeat(d_skip.astype(F32), inner // nh), norm_g)
    return matmul_norm_resid([y.reshape(b * t, inner)], w_out, gn[1], xf)


def _trunk(x, mem, norm_g, wts):
    b, t, d = x.shape
    nm = mem.shape[1]
    xf = x.reshape(b * t, d)
    memf = mem.reshape(b * nm, d)
    depth = norm_g.shape[0]
    for i in range(depth):
        gn = norm_g[i]
        j = i // 2
        if i % 2 == 0:
            xf = _even_mixer(xf, b, t, gn, wts["ev_w_in"][j], wts["ev_q_gain"][j], wts["ev_k_gain"][j],
                             wts["ev_w_out"][j])
        else:
            xf = _odd_mixer(xf, b, t, gn, wts["od_w_in_main"][j], wts["od_w_in_dt"][j], wts["od_conv_w"][j],
                            wts["od_conv_b"][j], wts["od_a_log"][j], wts["od_dt_bias"][j], wts["od_d"][j],
                            wts["od_norm_g"][j], wts["od_w_out"][j])
        q = norm_matmul(xf, gn[2], wts["xa_wq"][i]).reshape(b, t, d)
        kv = norm_matmul(memf, gn[4], wts["xa_wkv"][i]).reshape(b, nm, 2 * d)
        xa = cross_attention(q, kv).reshape(b * t, d)
        xf = matmul_norm_resid([xa], wts["xa_wo"][i], gn[3], xf)
        act = norm_swiglu(xf, gn[5], wts["ffn_w_gu"][i])
        xf = matmul_norm_resid([act], wts["ffn_w_down"][i], gn[6], xf)
    return xf.reshape(b, t, d)


def kernel(x_prompt, x_sample, mem_prompt, mem_sample, norm_g, ev_w_in, ev_q_gain, ev_k_gain, ev_w_out,
           od_w_in, od_conv_w, od_conv_b, od_a_log, od_dt_bias, od_d, od_norm_g, od_w_out,
           xa_wq, xa_wkv, xa_wo, ffn_w_gu, ffn_w_down):
    nh2 = od_a_log.shape[1] * od_a_log.shape[2]
    main = od_w_in.shape[2] - nh2
    wts = dict(
        ev_w_in=ev_w_in.astype(BF16), ev_q_gain=ev_q_gain, ev_k_gain=ev_k_gain,
        ev_w_out=ev_w_out.astype(BF16),
        od_w_in_main=od_w_in[:, :, :main].astype(BF16), od_w_in_dt=od_w_in[:, :, main:].astype(BF16),
        od_conv_w=od_conv_w, od_conv_b=od_conv_b, od_a_log=od_a_log, od_dt_bias=od_dt_bias,
        od_d=od_d, od_norm_g=od_norm_g, od_w_out=od_w_out.astype(BF16),
        xa_wq=xa_wq.astype(BF16), xa_wkv=xa_wkv.astype(BF16), xa_wo=xa_wo.astype(BF16),
        ffn_w_gu=ffn_w_gu.astype(BF16), ffn_w_down=ffn_w_down.astype(BF16),
    )
    y_prompt = _trunk(x_prompt, mem_prompt, norm_g, wts)
    y_sample = _trunk(x_sample, mem_sample, norm_g, wts)
    return (y_prompt, y_sample)
```

```python
import functools

import jax
import jax.numpy as jnp
from jax import lax
from jax.experimental import pallas as pl
from jax.experimental.pallas import tpu as pltpu

F32 = jnp.float32
BF16 = jnp.bfloat16

EPS = 1e-6
ROPE_THETA = 10000.0
GRID_W = 64

RET_HEADS = 4
RET_DK = 256
RET_DV = 256
RET_DECAY_EXP_FWD = 5.0
RET_DECAY_EXP_BWD = 5.5
ATT_HEADS = 8
ATT_KV_HEADS = 2
ATT_HD = 128
ATT_GROUP = ATT_HEADS // ATT_KV_HEADS

SSD_HEADDIM = 64
SSD_STATE = 128
SSD_GROUPS = 8
SSD_CONV = 5
SSD_CHUNK = 128

XA_HEADS = 4

VMEM_LIMIT_BYTES = 56 * 1024 * 1024
LANES = 128
SUBLANES = 8

RET_CHUNK = 256
ATT_BLOCK_Q = 128
XA_BLOCK_Q = 512
CONV_ROWS = 256
NEG_BIG = -1e30
RESIDENT_WEIGHT_BYTES = 24 * 1024 * 1024
N_CHUNK = 512
ATT_KV_BLOCKS = 4
LOG2E = 1.4426950408889634


def _params(*sem):
    return pltpu.CompilerParams(dimension_semantics=sem, vmem_limit_bytes=VMEM_LIMIT_BYTES)


def _dot(a, b):
    return jnp.dot(a, b, preferred_element_type=F32)


def _dot_nt(a, b):
    return lax.dot_general(a, b, (((1,), (1,)), ((), ())), preferred_element_type=F32)


def _dot_tn(a, b):
    return lax.dot_general(a, b, (((0,), (0,)), ((), ())), preferred_element_type=F32)


def _silu(x):
    h = 0.5 * x
    return h + h * jnp.tanh(h)


def _rms(x, g):
    return x * lax.rsqrt(jnp.mean(x * x, axis=-1, keepdims=True) + EPS) * g


def _pick_tile(n, pref):
    t = min(pref, n)
    while n % t:
        t //= 2
    return t


def _resident(shape):
    zeros = (0,) * len(shape)
    return pl.BlockSpec(shape, lambda *_: zeros, pipeline_mode=pl.Buffered(1))


def _nbytes(a):
    return a.size * a.dtype.itemsize


def _norm_matmul_resident_kernel(x_ref, g_ref, w_ref, o_ref, h_ref, *, n_chunk):
    h_ref[...] = _rms(x_ref[...], g_ref[...]).astype(BF16)
    for c0 in range(0, w_ref.shape[1], n_chunk):
        cols = slice(c0, c0 + n_chunk)
        o_ref[:, cols] = _dot(h_ref[...], w_ref[:, cols]).astype(o_ref.dtype)


def _norm_matmul_kernel(x_ref, g_ref, w_ref, *rest, has_extra):
    if has_extra:
        w2_ref, o_ref, o2_ref, h_ref = rest
    else:
        o_ref, h_ref = rest

    @pl.when(pl.program_id(1) == 0)
    def _():
        h_ref[...] = _rms(x_ref[...], g_ref[...]).astype(BF16)
        if has_extra:
            o2_ref[...] = _dot(h_ref[...], w2_ref[...])

    o_ref[...] = _dot(h_ref[...], w_ref[...]).astype(o_ref.dtype)


def norm_matmul(x, g, w, w_extra=None):
    m, d = x.shape
    n = w.shape[1]
    g2 = g.reshape(1, d).astype(F32)
    if w_extra is None and _nbytes(w) <= RESIDENT_WEIGHT_BYTES:
        tm = _pick_tile(m, 1024 if _nbytes(w) <= RESIDENT_WEIGHT_BYTES // 3 else 512)
        return pl.pallas_call(
            functools.partial(_norm_matmul_resident_kernel, n_chunk=_pick_tile(n, N_CHUNK)),
            grid=(m // tm,),
            in_specs=[pl.BlockSpec((tm, d), lambda i: (i, 0)), _resident((1, d)), _resident((d, n))],
            out_specs=pl.BlockSpec((tm, n), lambda i: (i, 0)),
            out_shape=jax.ShapeDtypeStruct((m, n), BF16),
            scratch_shapes=[pltpu.VMEM((tm, d), BF16)],
            compiler_params=_params("parallel"),
            name="norm_matmul_resident",
        )(x, g2, w)
    tm = _pick_tile(m, 1024)
    tn = _pick_tile(n, 1024)
    in_specs = [
        pl.BlockSpec((tm, d), lambda i, j: (i, 0)),
        _resident((1, d)),
        pl.BlockSpec((d, tn), lambda i, j: (0, j)),
    ]
    out_specs = pl.BlockSpec((tm, tn), lambda i, j: (i, j))
    out_shape = jax.ShapeDtypeStruct((m, n), BF16)
    args = (x, g2, w)
    if w_extra is not None:
        n2 = w_extra.shape[1]
        in_specs.append(_resident((d, n2)))
        out_specs = [out_specs, pl.BlockSpec((tm, n2), lambda i, j: (i, 0))]
        out_shape = [out_shape, jax.ShapeDtypeStruct((m, n2), F32)]
        args = args + (w_extra,)
    return pl.pallas_call(
        functools.partial(_norm_matmul_kernel, has_extra=w_extra is not None),
        grid=(m // tm, n // tn),
        in_specs=in_specs,
        out_specs=out_specs,
        out_shape=out_shape,
        scratch_shapes=[pltpu.VMEM((tm, d), BF16)],
        compiler_params=_params("parallel", "arbitrary"),
        name="norm_matmul",
    )(*args)


def _norm_swiglu_kernel(x_ref, g_ref, wg_ref, wu_ref, o_ref, h_ref):
    @pl.when(pl.program_id(1) == 0)
    def _():
        h_ref[...] = _rms(x_ref[...], g_ref[...]).astype(BF16)

    h = h_ref[...]
    gate = _dot(h, wg_ref[...])
    up = _dot(h, wu_ref[...])
    o_ref[...] = (_silu(gate) * up).astype(o_ref.dtype)


def norm_swiglu(x, g, w_gu, tm=1024, tn=512):
    m, d = x.shape
    f = w_gu.shape[1] // 2
    tm = _pick_tile(m, tm)
    tn = _pick_tile(f, tn)
    nj = f // tn
    return pl.pallas_call(
        _norm_swiglu_kernel,
        grid=(m // tm, nj),
        in_specs=[
            pl.BlockSpec((tm, d), lambda i, j: (i, 0)),
            pl.BlockSpec((1, d), lambda i, j: (0, 0)),
            pl.BlockSpec((d, tn), lambda i, j: (0, j)),
            pl.BlockSpec((d, tn), lambda i, j: (0, j + nj)),
        ],
        out_specs=pl.BlockSpec((tm, tn), lambda i, j: (i, j)),
        out_shape=jax.ShapeDtypeStruct((m, f), BF16),
        scratch_shapes=[pltpu.VMEM((tm, d), BF16)],
        compiler_params=_params("parallel", "arbitrary"),
        name="norm_swiglu",
    )(x, g.reshape(1, d).astype(F32), w_gu, w_gu)


def _matmul_norm_resid_kernel(*refs, n_a, n_chunk):
    a_refs = refs[:n_a]
    w_refs = refs[n_a:2 * n_a]
    g_ref, x_ref, o_ref = refs[2 * n_a:]
    tm, d = o_ref.shape
    ss = jnp.zeros((tm, 1), F32)
    for c0 in range(0, d, n_chunk):
        cols = slice(c0, c0 + n_chunk)
        y = _dot(a_refs[0][...], w_refs[0][:, cols])
        for a_ref, w_ref in zip(a_refs[1:], w_refs[1:]):
            y = y + _dot(a_ref[...], w_ref[:, cols])
        o_ref[:, cols] = y
        ss = ss + jnp.sum(y * y, axis=-1, keepdims=True)
    scale = lax.rsqrt(ss / d + EPS)
    for c0 in range(0, d, n_chunk):
        cols = slice(c0, c0 + n_chunk)
        o_ref[:, cols] = x_ref[:, cols] + o_ref[:, cols] * scale * g_ref[:, cols]


def matmul_norm_resid(parts, w, g, x, tm=512):
    m = parts[0].shape[0]
    d = w.shape[1]
    tm = _pick_tile(m, tm)
    n_a = len(parts)
    kp = parts[0].shape[1]
    assert all(p.shape[1] == kp for p in parts) and n_a * kp == w.shape[0]
    assert _nbytes(w) <= RESIDENT_WEIGHT_BYTES
    in_specs = [pl.BlockSpec((tm, kp), lambda i: (i, 0)) for _ in parts]
    in_specs += [pl.BlockSpec((kp, d), lambda i, r=r: (r, 0), pipeline_mode=pl.Buffered(1)) for r in range(n_a)]
    in_specs += [_resident((1, d)), pl.BlockSpec((tm, d), lambda i: (i, 0))]
    return pl.pallas_call(
        functools.partial(_matmul_norm_resid_kernel, n_a=n_a, n_chunk=_pick_tile(d, N_CHUNK)),
        grid=(m // tm,),
        in_specs=in_specs,
        out_specs=pl.BlockSpec((tm, d), lambda i: (i, 0)),
        out_shape=jax.ShapeDtypeStruct((m, d), F32),
        compiler_params=_params("parallel"),
        name="matmul_norm_resid",
    )(*parts, *([w] * n_a), g.reshape(1, d).astype(F32), x)


def _rope_angles(pos, dim):
    inv = ROPE_THETA ** (-jnp.arange(0, dim, 2, dtype=F32) / dim)
    ang = pos.astype(F32)[:, None] * inv[None, :]
    return jnp.cos(ang), jnp.sin(ang)


def _retention_tables(t):
    return _rope_angles(jnp.arange(t), RET_DK)


def _axial_tables(t):
    rows = t // GRID_W
    row = jnp.repeat(jnp.arange(rows), GRID_W)
    col = jnp.tile(jnp.arange(GRID_W), rows)
    half = ATT_HD // 2
    cos_r, sin_r = _rope_angles(row, half)
    cos_c, sin_c = _rope_angles(col, half)
    cos = jnp.concatenate([cos_r, cos_r, cos_c, cos_c], axis=-1)
    sin = jnp.concatenate([-sin_r, sin_r, -sin_c, sin_c], axis=-1)
    return cos, sin


def _retention_kernel(lg_ref, q_ref, k_ref, v_ref, g_ref, cos_ref, sin_ref, o_ref,
                      qs_ref, ks_ref, acc_ref, st_ref, *, chunk):
    t = q_ref.shape[1]
    dk = q_ref.shape[2]
    hk = dk // 2
    nc = t // chunk
    h = pl.program_id(1)
    lg_f = lg_ref[h, 0]
    lg_b = lg_ref[h, 1]

    def rope(x, cos, sin):
        x1 = x[:, :hk]
        x2 = x[:, hk:]
        return jnp.concatenate([x1 * cos - x2 * sin, x2 * cos + x1 * sin], axis=-1)

    def prep(c, carry):
        rows = pl.ds(pl.multiple_of(c * chunk, chunk), chunk)
        qs_ref[rows, :] = rope(q_ref[0, rows, :].astype(F32), cos_ref[rows, :], sin_ref[rows, :])
        ks_ref[rows, :] = rope(k_ref[0, rows, :].astype(F32), cos_ref[rows, :], sin_ref[rows, :]) * (dk ** -0.5)
        return carry

    lax.fori_loop(0, nc, prep, 0)

    idx = lax.broadcasted_iota(jnp.int32, (chunk, 1), 0).astype(F32)
    ri = lax.broadcasted_iota(jnp.int32, (chunk, chunk), 0)
    ci = lax.broadcasted_iota(jnp.int32, (chunk, chunk), 1)
    diff = (ri - ci).astype(F32)
    d_f = jnp.where(ri >= ci, jnp.exp(lg_f * jnp.where(ri >= ci, diff, 0.0)), 0.0)
    d_b = jnp.where(ri < ci, jnp.exp(lg_b * jnp.where(ri < ci, -diff, 0.0)), 0.0)
    xi_f = jnp.exp(lg_f * (idx + 1.0))
    zeta_f = jnp.exp(lg_f * (chunk - 1.0 - idx))
    xi_b = jnp.exp(lg_b * (chunk - idx))
    zeta_b = jnp.exp(lg_b * idx)
    one = jnp.ones((1, 1), F32)
    gc_f = jnp.exp(one * (lg_f * chunk))
    gc_b = jnp.exp(one * (lg_b * chunk))

    def scores(rows):
        qc = qs_ref[rows, :]
        kc = ks_ref[rows, :]
        kb = kc.astype(BF16)
        return qc, kb, _dot_nt(qc.astype(BF16), kb)

    def step(d, rows, dmat, xi, zeta, gc):
        qc, kb, s = scores(rows)
        vc = v_ref[0, rows, :]
        inner = _dot((s * dmat).astype(BF16), vc)
        cross = _dot((qc * xi).astype(BF16), st_ref[d].astype(BF16))
        st_ref[d] = gc * st_ref[d] + _dot_tn(kb, (vc.astype(F32) * zeta).astype(BF16))
        acc_ref[rows, :] += inner + cross

    acc_ref[...] = jnp.zeros_like(acc_ref)
    st_ref[...] = jnp.zeros_like(st_ref)

    def scan(i, carry):
        step(0, pl.ds(pl.multiple_of(i * chunk, chunk), chunk), d_f, xi_f, zeta_f, gc_f)
        step(1, pl.ds(pl.multiple_of((nc - 1 - i) * chunk, chunk), chunk), d_b, xi_b, zeta_b, gc_b)
        return carry

    lax.fori_loop(0, nc, scan, 0)

    def finish(c, carry):
        rows = pl.ds(pl.multiple_of(c * chunk, chunk), chunk)
        ret = acc_ref[rows, :]
        mu = jnp.mean(ret, axis=-1, keepdims=True)
        cen = ret - mu
        var = jnp.mean(cen * cen, axis=-1, keepdims=True)
        gate = _silu(g_ref[0, rows, :].astype(F32))
        o_ref[0, rows, :] = (gate * (cen * lax.rsqrt(var + EPS))).astype(o_ref.dtype)
        return carry

    lax.fori_loop(0, nc, finish, 0)


def retention(proj, lg, cos, sin):
    b, t, _ = proj.shape
    hh = RET_HEADS
    chunk = _pick_tile(t, RET_CHUNK)
    blk = lambda off: pl.BlockSpec((1, t, RET_DK), lambda bi, hi: (bi, 0, off + hi))
    return pl.pallas_call(
        functools.partial(_retention_kernel, chunk=chunk),
        grid=(b, hh),
        in_specs=[
            pl.BlockSpec(memory_space=pltpu.SMEM),
            blk(0), blk(hh), blk(2 * hh), blk(3 * hh),
            pl.BlockSpec((t, RET_DK // 2), lambda bi, hi: (0, 0)),
            pl.BlockSpec((t, RET_DK // 2), lambda bi, hi: (0, 0)),
        ],
        out_specs=pl.BlockSpec((1, t, RET_DV), lambda bi, hi: (bi, 0, hi)),
        out_shape=jax.ShapeDtypeStruct((b, t, hh * RET_DV), BF16),
        scratch_shapes=[
            pltpu.VMEM((t, RET_DK), F32),
            pltpu.VMEM((t, RET_DK), F32),
            pltpu.VMEM((t, RET_DV), F32),
            pltpu.VMEM((2, RET_DK, RET_DV), F32),
        ],
        compiler_params=_params("parallel", "parallel"),
        name="retention",
    )(lg, proj, proj, proj, proj, cos, sin)


def _axial_rope(x, cos, sin_signed):
    quarter = x.shape[-1] // 4
    lane = lax.broadcasted_iota(jnp.int32, x.shape, 1)
    first = (lane % (2 * quarter)) < quarter
    partner = jnp.where(first,
                        pltpu.roll(x, x.shape[-1] - quarter, axis=1),
                        pltpu.roll(x, quarter, axis=1))
    return x * cos + partner * sin_signed


def _gqa_kernel(q_ref, k_ref, v_ref, qg_ref, kg_ref, cosq_ref, sinq_ref, cosk_ref, sink_ref,
                o_ref, kt_ref):
    bq = q_ref.shape[1]
    hd = k_ref.shape[2]
    grp = q_ref.shape[2] // hd

    @pl.when(pl.program_id(2) == 0)
    def _():
        kn = _rms(k_ref[0].astype(F32), kg_ref[...])
        kt_ref[...] = _axial_rope(kn, cosk_ref[...], sink_ref[...]).astype(BF16)

    cos = cosq_ref[...]
    sin = sinq_ref[...]
    qs = []
    for h in range(grp):
        qn = _rms(q_ref[0, :, h * hd:(h + 1) * hd].astype(F32), qg_ref[...])
        qs.append((_axial_rope(qn, cos, sin) * (hd ** -0.5 * LOG2E)).astype(BF16))
    q = jnp.concatenate(qs, axis=0)
    t = kt_ref.shape[0]
    kb = t // _pick_tile(t // LANES, ATT_KV_BLOCKS)
    blocks = [slice(j, j + kb) for j in range(0, t, kb)]
    m = l = acc = None
    for blk in blocks:
        s = _dot_nt(q, kt_ref[blk, :])
        m_blk = jnp.max(s, axis=-1, keepdims=True)
        if m is None:
            m = m_blk
            p = jnp.exp2(s - m)
            l = jnp.sum(p, axis=-1, keepdims=True)
            acc = _dot(p.astype(BF16), v_ref[0, blk, :])
        else:
            m_new = jnp.maximum(m, m_blk)
            alpha = jnp.exp2(m - m_new)
            p = jnp.exp2(s - m_new)
            l = l * alpha + jnp.sum(p, axis=-1, keepdims=True)
            acc = acc * alpha + _dot(p.astype(BF16), v_ref[0, blk, :])
            m = m_new
    o = acc / l
    for h in range(grp):
        o_ref[0, :, h * hd:(h + 1) * hd] = o[h * bq:(h + 1) * bq].astype(o_ref.dtype)


def gqa_attention(proj, q_off, k_off, v_off, q_gain, k_gain, cos, sin):
    b, t, _ = proj.shape
    hd = ATT_HD
    gw = ATT_GROUP * hd
    bq = _pick_tile(t, ATT_BLOCK_Q)
    return pl.pallas_call(
        _gqa_kernel,
        grid=(b, ATT_KV_HEADS, t // bq),
        in_specs=[
            pl.BlockSpec((1, bq, gw), lambda bi, ki, qi: (bi, qi, q_off + ki)),
            pl.BlockSpec((1, t, hd), lambda bi, ki, qi: (bi, 0, k_off + ki)),
            pl.BlockSpec((1, t, hd), lambda bi, ki, qi: (bi, 0, v_off + ki)),
            pl.BlockSpec((1, hd), lambda bi, ki, qi: (0, 0)),
            pl.BlockSpec((1, hd), lambda bi, ki, qi: (0, 0)),
            pl.BlockSpec((bq, hd), lambda bi, ki, qi: (qi, 0)),
            pl.BlockSpec((bq, hd), lambda bi, ki, qi: (qi, 0)),
            pl.BlockSpec((t, hd), lambda bi, ki, qi: (0, 0)),
            pl.BlockSpec((t, hd), lambda bi, ki, qi: (0, 0)),
        ],
        out_specs=pl.BlockSpec((1, bq, gw), lambda bi, ki, qi: (bi, qi, ki)),
        out_shape=jax.ShapeDtypeStruct((b, t, ATT_HEADS * hd), BF16),
        scratch_shapes=[pltpu.VMEM((t, hd), BF16)],
        compiler_params=_params("parallel", "parallel", "arbitrary"),
        name="gqa_attention",
    )(proj, proj, proj, q_gain.reshape(1, hd).astype(F32), k_gain.reshape(1, hd).astype(F32),
      cos, sin, cos, sin)


def _cross_attn_kernel(q_ref, kv_ref, o_ref, *, heads):
    d = q_ref.shape[2]
    hd = d // heads
    for h in range(heads):
        qh = q_ref[0, :, h * hd:(h + 1) * hd]
        kh = kv_ref[0, :, h * hd:(h + 1) * hd]
        vh = kv_ref[0, :, d + h * hd:d + (h + 1) * hd]
        s = _dot_nt(qh, kh) * (hd ** -0.5)
        m = jnp.max(s, axis=-1, keepdims=True)
        p = jnp.exp(s - m)
        l = jnp.sum(p, axis=-1, keepdims=True)
        o_ref[0, :, h * hd:(h + 1) * hd] = (_dot(p.astype(BF16), vh) / l).astype(o_ref.dtype)


def cross_attention(q, kv):
    b, t, d = q.shape
    nm = kv.shape[1]
    tq = _pick_tile(t, XA_BLOCK_Q)
    return pl.pallas_call(
        functools.partial(_cross_attn_kernel, heads=XA_HEADS),
        grid=(b, t // tq),
        in_specs=[
            pl.BlockSpec((1, tq, d), lambda bi, qi: (bi, qi, 0)),
            pl.BlockSpec((1, nm, 2 * d), lambda bi, qi: (bi, 0, 0)),
        ],
        out_specs=pl.BlockSpec((1, tq, d), lambda bi, qi: (bi, qi, 0)),
        out_shape=jax.ShapeDtypeStruct((b, t, d), BF16),
        compiler_params=_params("parallel", "parallel"),
        name="cross_attention",
    )(q, kv)


def _conv_silu_kernel(x_ref, w_ref, b_ref, o_ref, pad_ref, *, rows):
    t = x_ref.shape[1]
    width = x_ref.shape[2]
    halo = SUBLANES
    taps = w_ref.shape[0]
    zeros = jnp.zeros((halo, width), F32)
    pad_ref[0:halo, :] = zeros
    pad_ref[halo + t:2 * halo + t, :] = zeros
    pad_ref[halo:halo + t, :] = x_ref[0].astype(F32)
    w = w_ref[...]
    bias = b_ref[...]

    def body(i, carry):
        r0 = pl.multiple_of(i * rows, rows)
        win = pad_ref[pl.ds(r0, rows + 2 * halo), :]
        acc = jnp.zeros((rows, width), F32)
        for k in range(taps):
            shift = (taps // 2 - k) % (rows + 2 * halo)
            shifted = win if shift == 0 else pltpu.roll(win, shift, axis=0)
            acc = acc + shifted[halo:halo + rows, :] * w[k:k + 1, :]
        o_ref[0, pl.ds(r0, rows), :] = _silu(acc + bias).astype(o_ref.dtype)
        return carry

    lax.fori_loop(0, t // rows, body, 0)


def conv_silu(zx, col_off, conv_w, conv_b, width=512):
    b, t, _ = zx.shape
    taps, c = conv_w.shape
    rows = _pick_tile(t, CONV_ROWS)
    return pl.pallas_call(
        functools.partial(_conv_silu_kernel, rows=rows),
        grid=(b, c // width),
        in_specs=[
            pl.BlockSpec((1, t, width), lambda bi, ci: (bi, 0, col_off + ci)),
            pl.BlockSpec((taps, width), lambda bi, ci: (0, ci)),
            pl.BlockSpec((1, width), lambda bi, ci: (0, ci)),
        ],
        out_specs=pl.BlockSpec((1, t, width), lambda bi, ci: (bi, 0, ci)),
        out_shape=jax.ShapeDtypeStruct((b, t, c), BF16),
        scratch_shapes=[pltpu.VMEM((t + 2 * SUBLANES, width), F32)],
        compiler_params=_params("parallel", "parallel"),
        name="conv_silu",
    )(zx, conv_w.astype(F32), conv_b.reshape(1, c).astype(F32))


def _dt_prep_kernel(raw_ref, bias_ref, alog_ref, dt_ref, acum_ref, *, chunk):
    t = raw_ref.shape[1]
    nh2 = raw_ref.shape[2]
    ri = lax.broadcasted_iota(jnp.int32, (chunk, chunk), 0)
    ci = lax.broadcasted_iota(jnp.int32, (chunk, chunk), 1)
    tri_f = (ci <= ri).astype(F32)
    tri_b = (ci >= ri).astype(F32)
    lane = lax.broadcasted_iota(jnp.int32, (chunk, nh2), 1)
    a = -jnp.exp(alog_ref[...])
    bias = bias_ref[...]

    def body(c, carry):
        rows = pl.ds(pl.multiple_of(c * chunk, chunk), chunk)
        dt = jax.nn.softplus(raw_ref[0, rows, :] + bias)
        dt_ref[0, rows, :] = dt
        da = dt * a
        pre = jnp.dot(tri_f, da, preferred_element_type=F32, precision=lax.Precision.HIGHEST)
        suf = jnp.dot(tri_b, da, preferred_element_type=F32, precision=lax.Precision.HIGHEST)
        acum_ref[0, rows, :] = jnp.where(lane < nh2 // 2, pre, suf)
        return carry

    lax.fori_loop(0, t // chunk, body, 0)


def dt_prep(dt_raw, dt_bias, a_log):
    b, t, nh2 = dt_raw.shape
    chunk = _pick_tile(t, SSD_CHUNK)
    spec = pl.BlockSpec((1, t, nh2), lambda bi: (bi, 0, 0))
    vec = pl.BlockSpec((1, nh2), lambda bi: (0, 0))
    return pl.pallas_call(
        functools.partial(_dt_prep_kernel, chunk=chunk),
        grid=(b,),
        in_specs=[spec, vec, vec],
        out_specs=[spec, spec],
        out_shape=[jax.ShapeDtypeStruct((b, t, nh2), F32)] * 2,
        compiler_params=_params("parallel"),
        name="ssd_dt_prep",
    )(dt_raw, dt_bias.reshape(1, nh2).astype(F32), a_log.reshape(1, nh2).astype(F32))


def _ssd_kernel(x_ref, b_ref, c_ref, z_ref, acum_ref, dt_ref, arow_ref, drow_ref, dskip_ref, ng_ref,
                o_ref, yacc_ref, st_ref, *, chunk):
    t = x_ref.shape[1]
    width = x_ref.shape[2]
    npair = width // LANES
    hp = LANES // 2
    heads = 2 * npair
    nc = t // chunk
    half_lanes = acum_ref.shape[2] // 2
    grp = pl.program_id(1)
    ri = lax.broadcasted_iota(jnp.int32, (chunk, chunk), 0)
    ci = lax.broadcasted_iota(jnp.int32, (chunk, chunk), 1)
    lane = lax.broadcasted_iota(jnp.int32, (chunk, LANES), 1)
    lo = lane < hp
    hi_i = (lane >= hp).astype(jnp.int32)

    def lane_take(a, idx):
        return jnp.take_along_axis(a, idx, axis=1, mode="promise_in_bounds")

    def dir_chunk(d, c):
        mask = (ri >= ci) if d == 0 else (ri <= ci)
        rows = pl.ds(pl.multiple_of(c * chunk, chunk), chunk)
        x = x_ref[0, rows, :].astype(F32)
        bc = b_ref[0, rows, :]
        cc = c_ref[0, rows, :]
        a_all = acum_ref[0, rows, :]
        dt_all = dt_ref[0, rows, :]
        a_end = a_all[chunk - 1:chunk, :] if d == 0 else a_all[0:1, :]
        ea_all = jnp.exp(a_all)
        w_all = jnp.exp(a_end - a_all) * dt_all
        ee_all = jnp.broadcast_to(jnp.exp(a_end), (SUBLANES, a_all.shape[1]))
        arow = arow_ref[0, d, 0, c]
        drow = drow_ref[0, d, 0, c]
        base = d * half_lanes + grp * heads

        cb = _dot_nt(cc, bc)
        coff = _dot(cc, st_ref[d].astype(BF16))

        ys, xws, ends = [], [], []
        for p in range(npair):
            pair_idx = base + 2 * p + hi_i
            xt = x[:, p * LANES:(p + 1) * LANES]
            y = coff[:, p * LANES:(p + 1) * LANES] * lane_take(ea_all, pair_idx)
            for half in range(2):
                k = 2 * p + half
                a_col = lane_take(a_all, jnp.zeros_like(lane) + (base + k))
                seg = a_col - arow[k:k + 1, :]
                mk = (cb * jnp.exp(jnp.where(mask, seg, NEG_BIG)) * drow[k:k + 1, :]).astype(BF16)
                keep = lo if half == 0 else jnp.logical_not(lo)
                y = y + _dot(mk, jnp.where(keep, xt, 0.0).astype(BF16))
            ys.append(y)
            xws.append((xt * lane_take(w_all, pair_idx)).astype(BF16))
            ends.append(lane_take(ee_all, pair_idx[:SUBLANES])[0:1])
        y = jnp.concatenate(ys, axis=-1)
        xw = jnp.concatenate(xws, axis=-1)
        e_row = jnp.concatenate(ends, axis=-1)
        st_ref[d] = st_ref[d] * e_row + _dot_tn(bc, xw)
        yacc_ref[rows, :] += y

    yacc_ref[...] = jnp.zeros_like(yacc_ref)
    st_ref[...] = jnp.zeros_like(st_ref)

    def scan(i, carry):
        dir_chunk(0, i)
        dir_chunk(1, nc - 1 - i)
        return carry

    lax.fori_loop(0, nc, scan, 0)

    def finish(c, carry):
        rows = pl.ds(pl.multiple_of(c * chunk, chunk), chunk)
        tot = yacc_ref[rows, :] + dskip_ref[...] * x_ref[0, rows, :].astype(F32)
        tot = tot * _silu(z_ref[0, rows, :].astype(F32))
        o_ref[0, rows, :] = _rms(tot, ng_ref[...]).astype(o_ref.dtype)
        return carry

    lax.fori_loop(0, nc, finish, 0)


def ssd_scan(xbc, zx, acum, dt, arow, drow, d_skip_wide, norm_g):
    b, t, _ = xbc.shape
    g = SSD_GROUPS
    n = SSD_STATE
    inner = d_skip_wide.shape[0]
    width = inner // g
    kh = width // SSD_HEADDIM
    chunk = _pick_tile(t, SSD_CHUNK)
    b_off = inner // n
    c_off = b_off + g
    nh2 = acum.shape[2]
    all_heads = pl.BlockSpec((1, t, nh2), lambda bi, gi: (bi, 0, 0))
    row_spec = pl.BlockSpec((1, 2, 1, t // chunk, kh, chunk), lambda bi, gi: (bi, 0, gi, 0, 0, 0))
    return pl.pallas_call(
        functools.partial(_ssd_kernel, chunk=chunk),
        grid=(b, g),
        in_specs=[
            pl.BlockSpec((1, t, width), lambda bi, gi: (bi, 0, gi)),
            pl.BlockSpec((1, t, n), lambda bi, gi: (bi, 0, b_off + gi)),
            pl.BlockSpec((1, t, n), lambda bi, gi: (bi, 0, c_off + gi)),
            pl.BlockSpec((1, t, width), lambda bi, gi: (bi, 0, gi)),
            all_heads, all_heads, row_spec, row_spec,
            pl.BlockSpec((1, width), lambda bi, gi: (0, gi)),
            pl.BlockSpec((1, width), lambda bi, gi: (0, gi)),
        ],
        out_specs=pl.BlockSpec((1, t, width), lambda bi, gi: (bi, 0, gi)),
        out_shape=jax.ShapeDtypeStruct((b, t, inner), BF16),
        scratch_shapes=[pltpu.VMEM((t, width), F32), pltpu.VMEM((2, n, width), F32)],
        compiler_params=_params("parallel", "arbitrary"),
        name="ssd_scan",
    )(xbc, xbc, xbc, zx, acum, dt, arow, drow,
      d_skip_wide.reshape(1, inner).astype(F32), norm_g.reshape(1, inner).astype(F32))


def _even_mixer(xf, b, t, gn, w_in, q_gain, k_gain, w_out):
    d = xf.shape[1]
    proj = norm_matmul(xf, gn[0], w_in).reshape(b, t, -1)
    heads = jnp.arange(RET_HEADS, dtype=F32)
    lg = jnp.stack([jnp.log1p(-jnp.exp2(-(RET_DECAY_EXP_FWD + heads))),
                    jnp.log1p(-jnp.exp2(-(RET_DECAY_EXP_BWD + heads)))], axis=1)
    cos1, sin1 = _retention_tables(t)
    ret = retention(proj, lg, cos1, sin1)
    cos2, sin2 = _axial_tables(t)
    att_base = 2 * RET_HEADS * RET_DK + 2 * RET_HEADS * RET_DV
    gw = ATT_GROUP * ATT_HD
    k_base = att_base + ATT_HEADS * ATT_HD
    v_base = k_base + ATT_KV_HEADS * ATT_HD
    att = gqa_attention(proj, att_base // gw, k_base // ATT_HD, v_base // ATT_HD,
                        q_gain, k_gain, cos2, sin2)
    parts = [ret.reshape(b * t, -1), att.reshape(b * t, -1)]
    return matmul_norm_resid(parts, w_out, gn[1], xf)


def _odd_mixer(xf, b, t, gn, w_in_main, w_in_dt, conv_w, conv_b, a_log, dt_bias, d_skip, norm_g, w_out):
    inner = norm_g.shape[0]
    nh = d_skip.shape[0]
    g = SSD_GROUPS
    kh = nh // g
    zx, dt_raw = norm_matmul(xf, gn[0], w_in_main, w_extra=w_in_dt)
    zx = zx.reshape(b, t, -1)
    dt_raw = dt_raw.reshape(b, t, 2 * nh)
    xbc = conv_silu(zx, inner // 512, conv_w, conv_b)
    dt, acum = dt_prep(dt_raw, dt_bias.reshape(-1), a_log.reshape(-1))
    chunk = _pick_tile(t, SSD_CHUNK)
    rows_of = lambda arr: arr.reshape(b, t // chunk, chunk, 2, g, kh).transpose(0, 3, 4, 1, 5, 2)
    y = ssd_scan(xbc, zx, acum, dt, rows_of(acum), rows_of(dt),
                 jnp.repeat(d_skip.astype(F32), inner // nh), norm_g)
    return matmul_norm_resid([y.reshape(b * t, inner)], w_out, gn[1], xf)


def _trunk(x, mem, norm_g, wts):
    b, t, d = x.shape
    nm = mem.shape[1]
    xf = x.reshape(b * t, d)
    memf = mem.reshape(b * nm, d)
    depth = norm_g.shape[0]
    for i in range(depth):
        gn = norm_g[i]
        j = i // 2
        if i % 2 == 0:
            xf = _even_mixer(xf, b, t, gn, wts["ev_w_in"][j], wts["ev_q_gain"][j], wts["ev_k_gain"][j],
                             wts["ev_w_out"][j])
        else:
            xf = _odd_mixer(xf, b, t, gn, wts["od_w_in_main"][j], wts["od_w_in_dt"][j], wts["od_conv_w"][j],
                            wts["od_conv_b"][j], wts["od_a_log"][j], wts["od_dt_bias"][j], wts["od_d"][j],
                            wts["od_norm_g"][j], wts["od_w_out"][j])
        q = norm_matmul(xf, gn[2], wts["xa_wq"][i]).reshape(b, t, d)
        kv = norm_matmul(memf, gn[4], wts["xa_wkv"][i]).reshape(b, nm, 2 * d)
        xa = cross_attention(q, kv).reshape(b * t, d)
        xf = matmul_norm_resid([xa], wts["xa_wo"][i], gn[3], xf)
        act = norm_swiglu(xf, gn[5], wts["ffn_w_gu"][i])
        xf = matmul_norm_resid([act], wts["ffn_w_down"][i], gn[6], xf)
    return xf.reshape(b, t, d)


def kernel(x_prompt, x_sample, mem_prompt, mem_sample, norm_g, ev_w_in, ev_q_gain, ev_k_gain, ev_w_out,
           od_w_in, od_conv_w, od_conv_b, od_a_log, od_dt_bias, od_d, od_norm_g, od_w_out,
           xa_wq, xa_wkv, xa_wo, ffn_w_gu, ffn_w_down):
    nh2 = od_a_log.shape[1] * od_a_log.shape[2]
    main = od_w_in.shape[2] - nh2
    wts = dict(
        ev_w_in=ev_w_in.astype(BF16), ev_q_gain=ev_q_gain, ev_k_gain=ev_k_gain,
        ev_w_out=ev_w_out.astype(BF16),
        od_w_in_main=od_w_in[:, :, :main].astype(BF16), od_w_in_dt=od_w_in[:, :, main:].astype(BF16),
        od_conv_w=od_conv_w, od_conv_b=od_conv_b, od_a_log=od_a_log, od_dt_bias=od_dt_bias,
        od_d=od_d, od_norm_g=od_norm_g, od_w_out=od_w_out.astype(BF16),
        xa_wq=xa_wq.astype(BF16), xa_wkv=xa_wkv.astype(BF16), xa_wo=xa_wo.astype(BF16),
        ffn_w_gu=ffn_w_gu.astype(BF16), ffn_w_down=ffn_w_down.astype(BF16),
    )
    y_prompt = _trunk(x_prompt, mem_prompt, norm_g, wts)
    y_sample = _trunk(x_sample, mem_sample, norm_g, wts)
    return (y_prompt, y_sample)
```

```python
import functools

import jax
import jax.numpy as jnp
from jax import lax
from jax.experimental import pallas as pl
from jax.experimental.pallas import tpu as pltpu

F32 = jnp.float32
BF16 = jnp.bfloat16

EPS = 1e-6
ROPE_THETA = 10000.0
GRID_W = 64

RET_HEADS = 4
RET_DK = 256
RET_DV = 256
RET_DECAY_EXP_FWD = 5.0
RET_DECAY_EXP_BWD = 5.5
ATT_HEADS = 8
ATT_KV_HEADS = 2
ATT_HD = 128
ATT_GROUP = ATT_HEADS // ATT_KV_HEADS

SSD_HEADDIM = 64
SSD_STATE = 128
SSD_GROUPS = 8
SSD_CONV = 5
SSD_CHUNK = 128

XA_HEADS = 4

VMEM_LIMIT_BYTES = 56 * 1024 * 1024
LANES = 128
SUBLANES = 8

RET_CHUNK = 256
ATT_BLOCK_Q = 128
XA_BLOCK_Q = 512
CONV_ROWS = 256
NEG_BIG = -1e30
RESIDENT_WEIGHT_BYTES = 24 * 1024 * 1024
N_CHUNK = 512
ATT_KV_BLOCKS = 2
LOG2E = 1.4426950408889634


def _params(*sem):
    return pltpu.CompilerParams(dimension_semantics=sem, vmem_limit_bytes=VMEM_LIMIT_BYTES)


def _dot(a, b):
    return jnp.dot(a, b, preferred_element_type=F32)


def _dot_nt(a, b):
    return lax.dot_general(a, b, (((1,), (1,)), ((), ())), preferred_element_type=F32)


def _dot_tn(a, b):
    return lax.dot_general(a, b, (((0,), (0,)), ((), ())), preferred_element_type=F32)


def _silu(x):
    h = 0.5 * x
    return h + h * jnp.tanh(h)


def _rms(x, g):
    return x * lax.rsqrt(jnp.mean(x * x, axis=-1, keepdims=True) + EPS) * g


def _pick_tile(n, pref):
    t = min(pref, n)
    while n % t:
        t //= 2
    return t


def _resident(shape):
    zeros = (0,) * len(shape)
    return pl.BlockSpec(shape, lambda *_: zeros, pipeline_mode=pl.Buffered(1))


def _nbytes(a):
    return a.size * a.dtype.itemsize


def _norm_matmul_resident_kernel(x_ref, g_ref, w_ref, o_ref, h_ref, *, n_chunk):
    h_ref[...] = _rms(x_ref[...], g_ref[...]).astype(BF16)
    for c0 in range(0, w_ref.shape[1], n_chunk):
        cols = slice(c0, c0 + n_chunk)
        o_ref[:, cols] = _dot(h_ref[...], w_ref[:, cols]).astype(o_ref.dtype)


def _norm_matmul_kernel(x_ref, g_ref, w_ref, *rest, has_extra):
    if has_extra:
        w2_ref, o_ref, o2_ref, h_ref = rest
    else:
        o_ref, h_ref = rest

    @pl.when(pl.program_id(1) == 0)
    def _():
        h_ref[...] = _rms(x_ref[...], g_ref[...]).astype(BF16)
        if has_extra:
            o2_ref[...] = _dot(h_ref[...], w2_ref[...])

    o_ref[...] = _dot(h_ref[...], w_ref[...]).astype(o_ref.dtype)


def norm_matmul(x, g, w, n_main=None):
    m, d = x.shape
    n = w.shape[1] if n_main is None else n_main
    n2 = w.shape[1] - n
    g2 = g.reshape(1, d).astype(F32)
    if n2 == 0 and _nbytes(w) <= RESIDENT_WEIGHT_BYTES:
        tm = _pick_tile(m, 1024 if _nbytes(w) <= RESIDENT_WEIGHT_BYTES // 3 else 512)
        return pl.pallas_call(
            functools.partial(_norm_matmul_resident_kernel, n_chunk=_pick_tile(n, N_CHUNK)),
            grid=(m // tm,),
            in_specs=[pl.BlockSpec((tm, d), lambda i: (i, 0)), _resident((1, d)), _resident((d, n))],
            out_specs=pl.BlockSpec((tm, n), lambda i: (i, 0)),
            out_shape=jax.ShapeDtypeStruct((m, n), BF16),
            scratch_shapes=[pltpu.VMEM((tm, d), BF16)],
            compiler_params=_params("parallel"),
            name="norm_matmul_resident",
        )(x, g2, w)
    tm = _pick_tile(m, 1024)
    tn = _pick_tile(n, 1024)
    in_specs = [
        pl.BlockSpec((tm, d), lambda i, j: (i, 0)),
        _resident((1, d)),
        pl.BlockSpec((d, tn), lambda i, j: (0, j)),
    ]
    out_specs = pl.BlockSpec((tm, tn), lambda i, j: (i, j))
    out_shape = jax.ShapeDtypeStruct((m, n), BF16)
    args = (x, g2, w)
    if n2:
        assert n % n2 == 0
        in_specs.append(pl.BlockSpec((d, n2), lambda i, j: (0, n // n2), pipeline_mode=pl.Buffered(1)))
        out_specs = [out_specs, pl.BlockSpec((tm, n2), lambda i, j: (i, 0))]
        out_shape = [out_shape, jax.ShapeDtypeStruct((m, n2), F32)]
        args = args + (w,)
    return pl.pallas_call(
        functools.partial(_norm_matmul_kernel, has_extra=bool(n2)),
        grid=(m // tm, n // tn),
        in_specs=in_specs,
        out_specs=out_specs,
        out_shape=out_shape,
        scratch_shapes=[pltpu.VMEM((tm, d), BF16)],
        compiler_params=_params("parallel", "arbitrary"),
        name="norm_matmul",
    )(*args)


def _norm_swiglu_kernel(x_ref, g_ref, wg_ref, wu_ref, o_ref, h_ref):
    @pl.when(pl.program_id(1) == 0)
    def _():
        h_ref[...] = _rms(x_ref[...], g_ref[...]).astype(BF16)

    h = h_ref[...]
    gate = _dot(h, wg_ref[...])
    up = _dot(h, wu_ref[...])
    o_ref[...] = (_silu(gate) * up).astype(o_ref.dtype)


def norm_swiglu(x, g, w_gu, tm=1024, tn=512):
    m, d = x.shape
    f = w_gu.shape[1] // 2
    tm = _pick_tile(m, tm)
    tn = _pick_tile(f, tn)
    nj = f // tn
    return pl.pallas_call(
        _norm_swiglu_kernel,
        grid=(m // tm, nj),
        in_specs=[
            pl.BlockSpec((tm, d), lambda i, j: (i, 0)),
            pl.BlockSpec((1, d), lambda i, j: (0, 0)),
            pl.BlockSpec((d, tn), lambda i, j: (0, j)),
            pl.BlockSpec((d, tn), lambda i, j: (0, j + nj)),
        ],
        out_specs=pl.BlockSpec((tm, tn), lambda i, j: (i, j)),
        out_shape=jax.ShapeDtypeStruct((m, f), BF16),
        scratch_shapes=[pltpu.VMEM((tm, d), BF16)],
        compiler_params=_params("parallel", "arbitrary"),
        name="norm_swiglu",
    )(x, g.reshape(1, d).astype(F32), w_gu, w_gu)


def _matmul_norm_resid_kernel(*refs, n_a, n_chunk):
    a_refs = refs[:n_a]
    w_refs = refs[n_a:2 * n_a]
    g_ref, x_ref, o_ref = refs[2 * n_a:]
    tm, d = o_ref.shape
    ss = jnp.zeros((tm, 1), F32)
    for c0 in range(0, d, n_chunk):
        cols = slice(c0, c0 + n_chunk)
        y = _dot(a_refs[0][...], w_refs[0][:, cols])
        for a_ref, w_ref in zip(a_refs[1:], w_refs[1:]):
            y = y + _dot(a_ref[...], w_ref[:, cols])
        o_ref[:, cols] = y
        ss = ss + jnp.sum(y * y, axis=-1, keepdims=True)
    scale = lax.rsqrt(ss / d + EPS)
    for c0 in range(0, d, n_chunk):
        cols = slice(c0, c0 + n_chunk)
        o_ref[:, cols] = x_ref[:, cols] + o_ref[:, cols] * scale * g_ref[:, cols]


def matmul_norm_resid(parts, w, g, x, tm=512):
    m = parts[0].shape[0]
    d = w.shape[1]
    tm = _pick_tile(m, tm)
    n_a = len(parts)
    kp = parts[0].shape[1]
    assert all(p.shape[1] == kp for p in parts) and n_a * kp == w.shape[0]
    assert _nbytes(w) <= RESIDENT_WEIGHT_BYTES
    in_specs = [pl.BlockSpec((tm, kp), lambda i: (i, 0)) for _ in parts]
    in_specs += [pl.BlockSpec((kp, d), lambda i, r=r: (r, 0), pipeline_mode=pl.Buffered(1)) for r in range(n_a)]
    in_specs += [_resident((1, d)), pl.BlockSpec((tm, d), lambda i: (i, 0))]
    return pl.pallas_call(
        functools.partial(_matmul_norm_resid_kernel, n_a=n_a, n_chunk=_pick_tile(d, N_CHUNK)),
        grid=(m // tm,),
        in_specs=in_specs,
        out_specs=pl.BlockSpec((tm, d), lambda i: (i, 0)),
        out_shape=jax.ShapeDtypeStruct((m, d), F32),
        compiler_params=_params("parallel"),
        name="matmul_norm_resid",
    )(*parts, *([w] * n_a), g.reshape(1, d).astype(F32), x)


def _rope_angles(pos, dim):
    inv = ROPE_THETA ** (-jnp.arange(0, dim, 2, dtype=F32) / dim)
    ang = pos.astype(F32)[:, None] * inv[None, :]
    return jnp.cos(ang), jnp.sin(ang)


def _retention_tables(t):
    return _rope_angles(jnp.arange(t), RET_DK)


def _axial_tables(t):
    rows = t // GRID_W
    row = jnp.repeat(jnp.arange(rows), GRID_W)
    col = jnp.tile(jnp.arange(GRID_W), rows)
    half = ATT_HD // 2
    cos_r, sin_r = _rope_angles(row, half)
    cos_c, sin_c = _rope_angles(col, half)
    cos = jnp.concatenate([cos_r, cos_r, cos_c, cos_c], axis=-1)
    sin = jnp.concatenate([-sin_r, sin_r, -sin_c, sin_c], axis=-1)
    return cos, sin


def _retention_kernel(lg_ref, q_ref, k_ref, v_ref, g_ref, cos_ref, sin_ref, o_ref,
                      qs_ref, ks_ref, acc_ref, st_ref, *, chunk):
    t = q_ref.shape[1]
    dk = q_ref.shape[2]
    hk = dk // 2
    nc = t // chunk
    h = pl.program_id(1)
    lg_f = lg_ref[h, 0]
    lg_b = lg_ref[h, 1]

    def rope(x, cos, sin):
        x1 = x[:, :hk]
        x2 = x[:, hk:]
        return jnp.concatenate([x1 * cos - x2 * sin, x2 * cos + x1 * sin], axis=-1)

    def prep(c, carry):
        rows = pl.ds(pl.multiple_of(c * chunk, chunk), chunk)
        qs_ref[rows, :] = rope(q_ref[0, rows, :].astype(F32), cos_ref[rows, :], sin_ref[rows, :])
        ks_ref[rows, :] = rope(k_ref[0, rows, :].astype(F32), cos_ref[rows, :], sin_ref[rows, :]) * (dk ** -0.5)
        return carry

    lax.fori_loop(0, nc, prep, 0)

    idx = lax.broadcasted_iota(jnp.int32, (chunk, 1), 0).astype(F32)
    ri = lax.broadcasted_iota(jnp.int32, (chunk, chunk), 0)
    ci = lax.broadcasted_iota(jnp.int32, (chunk, chunk), 1)
    diff = (ri - ci).astype(F32)
    d_f = jnp.where(ri >= ci, jnp.exp(lg_f * jnp.where(ri >= ci, diff, 0.0)), 0.0)
    d_b = jnp.where(ri < ci, jnp.exp(lg_b * jnp.where(ri < ci, -diff, 0.0)), 0.0)
    xi_f = jnp.exp(lg_f * (idx + 1.0))
    zeta_f = jnp.exp(lg_f * (chunk - 1.0 - idx))
    xi_b = jnp.exp(lg_b * (chunk - idx))
    zeta_b = jnp.exp(lg_b * idx)
    one = jnp.ones((1, 1), F32)
    gc_f = jnp.exp(one * (lg_f * chunk))
    gc_b = jnp.exp(one * (lg_b * chunk))

    def scores(rows):
        qc = qs_ref[rows, :]
        kc = ks_ref[rows, :]
        kb = kc.astype(BF16)
        return qc, kb, _dot_nt(qc.astype(BF16), kb)

    def step(d, rows, dmat, xi, zeta, gc):
        qc, kb, s = scores(rows)
        vc = v_ref[0, rows, :]
        inner = _dot((s * dmat).astype(BF16), vc)
        cross = _dot((qc * xi).astype(BF16), st_ref[d].astype(BF16))
        st_ref[d] = gc * st_ref[d] + _dot_tn(kb, (vc.astype(F32) * zeta).astype(BF16))
        acc_ref[rows, :] += inner + cross

    acc_ref[...] = jnp.zeros_like(acc_ref)
    st_ref[...] = jnp.zeros_like(st_ref)

    def scan(i, carry):
        step(0, pl.ds(pl.multiple_of(i * chunk, chunk), chunk), d_f, xi_f, zeta_f, gc_f)
        step(1, pl.ds(pl.multiple_of((nc - 1 - i) * chunk, chunk), chunk), d_b, xi_b, zeta_b, gc_b)
        return carry

    lax.fori_loop(0, nc, scan, 0, unroll=2)

    def finish(c, carry):
        rows = pl.ds(pl.multiple_of(c * chunk, chunk), chunk)
        ret = acc_ref[rows, :]
        mu = jnp.mean(ret, axis=-1, keepdims=True)
        cen = ret - mu
        var = jnp.mean(cen * cen, axis=-1, keepdims=True)
        gate = _silu(g_ref[0, rows, :].astype(F32))
        o_ref[0, rows, :] = (gate * (cen * lax.rsqrt(var + EPS))).astype(o_ref.dtype)
        return carry

    lax.fori_loop(0, nc, finish, 0)


def retention(proj, lg, cos, sin):
    b, t, _ = proj.shape
    hh = RET_HEADS
    chunk = _pick_tile(t, RET_CHUNK)
    blk = lambda off: pl.BlockSpec((1, t, RET_DK), lambda bi, hi: (bi, 0, off + hi))
    return pl.pallas_call(
        functools.partial(_retention_kernel, chunk=chunk),
        grid=(b, hh),
        in_specs=[
            pl.BlockSpec(memory_space=pltpu.SMEM),
            blk(0), blk(hh), blk(2 * hh), blk(3 * hh),
            pl.BlockSpec((t, RET_DK // 2), lambda bi, hi: (0, 0)),
            pl.BlockSpec((t, RET_DK // 2), lambda bi, hi: (0, 0)),
        ],
        out_specs=pl.BlockSpec((1, t, RET_DV), lambda bi, hi: (bi, 0, hi)),
        out_shape=jax.ShapeDtypeStruct((b, t, hh * RET_DV), BF16),
        scratch_shapes=[
            pltpu.VMEM((t, RET_DK), F32),
            pltpu.VMEM((t, RET_DK), F32),
            pltpu.VMEM((t, RET_DV), F32),
            pltpu.VMEM((2, RET_DK, RET_DV), F32),
        ],
        compiler_params=_params("parallel", "parallel"),
        name="retention",
    )(lg, proj, proj, proj, proj, cos, sin)


def _axial_rope(x, cos, sin_signed):
    quarter = x.shape[-1] // 4
    lane = lax.broadcasted_iota(jnp.int32, x.shape, 1)
    first = (lane % (2 * quarter)) < quarter
    partner = jnp.where(first,
                        pltpu.roll(x, x.shape[-1] - quarter, axis=1),
                        pltpu.roll(x, quarter, axis=1))
    return x * cos + partner * sin_signed


def _gqa_kernel(q_ref, k_ref, v_ref, qg_ref, kg_ref, cosq_ref, sinq_ref, cosk_ref, sink_ref,
                o_ref, kt_ref):
    bq = q_ref.shape[1]
    hd = k_ref.shape[2]
    grp = q_ref.shape[2] // hd

    @pl.when(pl.program_id(2) == 0)
    def _():
        kn = _rms(k_ref[0].astype(F32), kg_ref[...])
        kt_ref[...] = _axial_rope(kn, cosk_ref[...], sink_ref[...]).astype(BF16)

    cos = cosq_ref[...]
    sin = sinq_ref[...]
    qs = []
    for h in range(grp):
        qn = _rms(q_ref[0, :, h * hd:(h + 1) * hd].astype(F32), qg_ref[...])
        qs.append((_axial_rope(qn, cos, sin) * (hd ** -0.5 * LOG2E)).astype(BF16))
    q = jnp.concatenate(qs, axis=0)
    t = kt_ref.shape[0]
    kb = t // _pick_tile(t // LANES, ATT_KV_BLOCKS)
    blocks = [slice(j, j + kb) for j in range(0, t, kb)]
    m = l = acc = None
    for blk in blocks:
        s = _dot_nt(q, kt_ref[blk, :])
        m_blk = jnp.max(s, axis=-1, keepdims=True)
        if m is None:
            m = m_blk
            p = jnp.exp2(s - m)
            l = jnp.sum(p, axis=-1, keepdims=True)
            acc = _dot(p.astype(BF16), v_ref[0, blk, :])
        else:
            m_new = jnp.maximum(m, m_blk)
            alpha = jnp.exp2(m - m_new)
            p = jnp.exp2(s - m_new)
            l = l * alpha + jnp.sum(p, axis=-1, keepdims=True)
            acc = acc * alpha + _dot(p.astype(BF16), v_ref[0, blk, :])
            m = m_new
    o = acc / l
    for h in range(grp):
        o_ref[0, :, h * hd:(h + 1) * hd] = o[h * bq:(h + 1) * bq].astype(o_ref.dtype)


def gqa_attention(proj, q_off, k_off, v_off, q_gain, k_gain, cos, sin):
    b, t, _ = proj.shape
    hd = ATT_HD
    gw = ATT_GROUP * hd
    bq = _pick_tile(t, ATT_BLOCK_Q)
    return pl.pallas_call(
        _gqa_kernel,
        grid=(b, ATT_KV_HEADS, t // bq),
        in_specs=[
            pl.BlockSpec((1, bq, gw), lambda bi, ki, qi: (bi, qi, q_off + ki)),
            pl.BlockSpec((1, t, hd), lambda bi, ki, qi: (bi, 0, k_off + ki)),
            pl.BlockSpec((1, t, hd), lambda bi, ki, qi: (bi, 0, v_off + ki)),
            pl.BlockSpec((1, hd), lambda bi, ki, qi: (0, 0)),
            pl.BlockSpec((1, hd), lambda bi, ki, qi: (0, 0)),
            pl.BlockSpec((bq, hd), lambda bi, ki, qi: (qi, 0)),
            pl.BlockSpec((bq, hd), lambda bi, ki, qi: (qi, 0)),
            pl.BlockSpec((t, hd), lambda bi, ki, qi: (0, 0)),
            pl.BlockSpec((t, hd), lambda bi, ki, qi: (0, 0)),
        ],
        out_specs=pl.BlockSpec((1, bq, gw), lambda bi, ki, qi: (bi, qi, ki)),
        out_shape=jax.ShapeDtypeStruct((b, t, ATT_HEADS * hd), BF16),
        scratch_shapes=[pltpu.VMEM((t, hd), BF16)],
        compiler_params=_params("parallel", "parallel", "arbitrary"),
        name="gqa_attention",
    )(proj, proj, proj, q_gain.reshape(1, hd).astype(F32), k_gain.reshape(1, hd).astype(F32),
      cos, sin, cos, sin)


def _cross_attn_kernel(q_ref, kv_ref, o_ref, *, heads):
    d = q_ref.shape[2]
    hd = d // heads
    for h in range(heads):
        qh = q_ref[0, :, h * hd:(h + 1) * hd]
        kh = kv_ref[0, :, h * hd:(h + 1) * hd]
        vh = kv_ref[0, :, d + h * hd:d + (h + 1) * hd]
        s = _dot_nt(qh, kh) * (hd ** -0.5)
        m = jnp.max(s, axis=-1, keepdims=True)
        p = jnp.exp(s - m)
        l = jnp.sum(p, axis=-1, keepdims=True)
        o_ref[0, :, h * hd:(h + 1) * hd] = (_dot(p.astype(BF16), vh) / l).astype(o_ref.dtype)


def cross_attention(q, kv):
    b, t, d = q.shape
    nm = kv.shape[1]
    tq = _pick_tile(t, XA_BLOCK_Q)
    return pl.pallas_call(
        functools.partial(_cross_attn_kernel, heads=XA_HEADS),
        grid=(b, t // tq),
        in_specs=[
            pl.BlockSpec((1, tq, d), lambda bi, qi: (bi, qi, 0)),
            pl.BlockSpec((1, nm, 2 * d), lambda bi, qi: (bi, 0, 0)),
        ],
        out_specs=pl.BlockSpec((1, tq, d), lambda bi, qi: (bi, qi, 0)),
        out_shape=jax.ShapeDtypeStruct((b, t, d), BF16),
        compiler_params=_params("parallel", "parallel"),
        name="cross_attention",
    )(q, kv)


def _conv_silu_kernel(x_ref, w_ref, b_ref, o_ref, pad_ref, *, rows):
    t = x_ref.shape[1]
    width = x_ref.shape[2]
    halo = SUBLANES
    taps = w_ref.shape[0]
    zeros = jnp.zeros((halo, width), F32)
    pad_ref[0:halo, :] = zeros
    pad_ref[halo + t:2 * halo + t, :] = zeros
    pad_ref[halo:halo + t, :] = x_ref[0].astype(F32)
    w = w_ref[...]
    bias = b_ref[...]

    def body(i, carry):
        r0 = pl.multiple_of(i * rows, rows)
        win = pad_ref[pl.ds(r0, rows + 2 * halo), :]
        acc = jnp.zeros((rows, width), F32)
        for k in range(taps):
            shift = (taps // 2 - k) % (rows + 2 * halo)
            shifted = win if shift == 0 else pltpu.roll(win, shift, axis=0)
            acc = acc + shifted[halo:halo + rows, :] * w[k:k + 1, :]
        o_ref[0, pl.ds(r0, rows), :] = _silu(acc + bias).astype(o_ref.dtype)
        return carry

    lax.fori_loop(0, t // rows, body, 0)


def conv_silu(zx, col_off, conv_w, conv_b, width=512):
    b, t, _ = zx.shape
    taps, c = conv_w.shape
    rows = _pick_tile(t, CONV_ROWS)
    return pl.pallas_call(
        functools.partial(_conv_silu_kernel, rows=rows),
        grid=(b, c // width),
        in_specs=[
            pl.BlockSpec((1, t, width), lambda bi, ci: (bi, 0, col_off + ci)),
            pl.BlockSpec((taps, width), lambda bi, ci: (0, ci)),
            pl.BlockSpec((1, width), lambda bi, ci: (0, ci)),
        ],
        out_specs=pl.BlockSpec((1, t, width), lambda bi, ci: (bi, 0, ci)),
        out_shape=jax.ShapeDtypeStruct((b, t, c), BF16),
        scratch_shapes=[pltpu.VMEM((t + 2 * SUBLANES, width), F32)],
        compiler_params=_params("parallel", "parallel"),
        name="conv_silu",
    )(zx, conv_w.astype(F32), conv_b.reshape(1, c).astype(F32))


def _dt_prep_kernel(raw_ref, bias_ref, alog_ref, dt_ref, acum_ref, *, chunk):
    t = raw_ref.shape[1]
    nh2 = raw_ref.shape[2]
    ri = lax.broadcasted_iota(jnp.int32, (chunk, chunk), 0)
    ci = lax.broadcasted_iota(jnp.int32, (chunk, chunk), 1)
    tri_f = (ci <= ri).astype(F32)
    tri_b = (ci >= ri).astype(F32)
    lane = lax.broadcasted_iota(jnp.int32, (chunk, nh2), 1)
    a = -jnp.exp(alog_ref[...])
    bias = bias_ref[...]

    def body(c, carry):
        rows = pl.ds(pl.multiple_of(c * chunk, chunk), chunk)
        dt = jax.nn.softplus(raw_ref[0, rows, :] + bias)
        dt_ref[0, rows, :] = dt
        da = dt * a
        pre = jnp.dot(tri_f, da, preferred_element_type=F32, precision=lax.Precision.HIGHEST)
        suf = jnp.dot(tri_b, da, preferred_element_type=F32, precision=lax.Precision.HIGHEST)
        acum_ref[0, rows, :] = jnp.where(lane < nh2 // 2, pre, suf)
        return carry

    lax.fori_loop(0, t // chunk, body, 0)


def dt_prep(dt_raw, dt_bias, a_log):
    b, t, nh2 = dt_raw.shape
    chunk = _pick_tile(t, SSD_CHUNK)
    spec = pl.BlockSpec((1, t, nh2), lambda bi: (bi, 0, 0))
    vec = pl.BlockSpec((1, nh2), lambda bi: (0, 0))
    return pl.pallas_call(
        functools.partial(_dt_prep_kernel, chunk=chunk),
        grid=(b,),
        in_specs=[spec, vec, vec],
        out_specs=[spec, spec],
        out_shape=[jax.ShapeDtypeStruct((b, t, nh2), F32)] * 2,
        compiler_params=_params("parallel"),
        name="ssd_dt_prep",
    )(dt_raw, dt_bias.reshape(1, nh2).astype(F32), a_log.reshape(1, nh2).astype(F32))


def _ssd_kernel(x_ref, b_ref, c_ref, z_ref, acum_ref, dt_ref, arow_ref, drow_ref, dskip_ref, ng_ref,
                o_ref, yacc_ref, st_ref, *, chunk):
    t = x_ref.shape[1]
    width = x_ref.shape[2]
    npair = width // LANES
    hp = LANES // 2
    heads = 2 * npair
    nc = t // chunk
    half_lanes = acum_ref.shape[2] // 2
    grp = pl.program_id(1)
    ri = lax.broadcasted_iota(jnp.int32, (chunk, chunk), 0)
    ci = lax.broadcasted_iota(jnp.int32, (chunk, chunk), 1)
    lane = lax.broadcasted_iota(jnp.int32, (chunk, LANES), 1)
    lo = lane < hp
    hi_i = (lane >= hp).astype(jnp.int32)

    def lane_take(a, idx):
        return jnp.take_along_axis(a, idx, axis=1, mode="promise_in_bounds")

    def dir_chunk(d, c):
        mask = (ri >= ci) if d == 0 else (ri <= ci)
        rows = pl.ds(pl.multiple_of(c * chunk, chunk), chunk)
        x = x_ref[0, rows, :].astype(F32)
        bc = b_ref[0, rows, :]
        cc = c_ref[0, rows, :]
        a_all = acum_ref[0, rows, :] * LOG2E
        dt_all = dt_ref[0, rows, :]
        a_end = a_all[chunk - 1:chunk, :] if d == 0 else a_all[0:1, :]
        w_all = jnp.exp2(a_end - a_all) * dt_all
        ee_all = jnp.broadcast_to(jnp.exp2(a_end), (SUBLANES, a_all.shape[1]))
        arow = arow_ref[0, d, 0, c] * LOG2E - jnp.log2(drow_ref[0, d, 0, c])
        base = d * half_lanes + grp * heads

        cb = _dot_nt(cc, bc)
        coff = _dot(cc, st_ref[d].astype(BF16))

        ys, xws, ends = [], [], []
        for p in range(npair):
            pair_idx = base + 2 * p + hi_i
            xt = x[:, p * LANES:(p + 1) * LANES]
            a_cols = [lane_take(a_all, jnp.zeros_like(lane) + (base + 2 * p + half)) for half in range(2)]
            y = coff[:, p * LANES:(p + 1) * LANES] * jnp.exp2(jnp.where(lo, a_cols[0], a_cols[1]))
            for half in range(2):
                k = 2 * p + half
                seg = a_cols[half] - arow[k:k + 1, :]
                mk = (cb * jnp.exp2(jnp.where(mask, seg, NEG_BIG))).astype(BF16)
                keep = lo if half == 0 else jnp.logical_not(lo)
                y = y + _dot(mk, jnp.where(keep, xt, 0.0).astype(BF16))
            ys.append(y)
            xws.append((xt * lane_take(w_all, pair_idx)).astype(BF16))
            ends.append(lane_take(ee_all, pair_idx[:SUBLANES])[0:1])
        y = jnp.concatenate(ys, axis=-1)
        xw = jnp.concatenate(xws, axis=-1)
        e_row = jnp.concatenate(ends, axis=-1)
        st_ref[d] = st_ref[d] * e_row + _dot_tn(bc, xw)
        yacc_ref[rows, :] += y

    yacc_ref[...] = jnp.zeros_like(yacc_ref)
    st_ref[...] = jnp.zeros_like(st_ref)

    def scan(i, carry):
        dir_chunk(0, i)
        dir_chunk(1, nc - 1 - i)
        return carry

    lax.fori_loop(0, nc, scan, 0, unroll=2)

    def finish(c, carry):
        rows = pl.ds(pl.multiple_of(c * chunk, chunk), chunk)
        tot = yacc_ref[rows, :] + dskip_ref[...] * x_ref[0, rows, :].astype(F32)
        tot = tot * _silu(z_ref[0, rows, :].astype(F32))
        o_ref[0, rows, :] = _rms(tot, ng_ref[...]).astype(o_ref.dtype)
        return carry

    lax.fori_loop(0, nc, finish, 0)


def ssd_scan(xbc, zx, acum, dt, arow, drow, d_skip_wide, norm_g):
    b, t, _ = xbc.shape
    g = SSD_GROUPS
    n = SSD_STATE
    inner = d_skip_wide.shape[0]
    width = inner // g
    kh = width // SSD_HEADDIM
    chunk = _pick_tile(t, SSD_CHUNK)
    b_off = inner // n
    c_off = b_off + g
    nh2 = acum.shape[2]
    all_heads = pl.BlockSpec((1, t, nh2), lambda bi, gi: (bi, 0, 0))
    row_spec = pl.BlockSpec((1, 2, 1, t // chunk, kh, chunk), lambda bi, gi: (bi, 0, gi, 0, 0, 0))
    return pl.pallas_call(
        functools.partial(_ssd_kernel, chunk=chunk),
        grid=(b, g),
        in_specs=[
            pl.BlockSpec((1, t, width), lambda bi, gi: (bi, 0, gi)),
            pl.BlockSpec((1, t, n), lambda bi, gi: (bi, 0, b_off + gi)),
            pl.BlockSpec((1, t, n), lambda bi, gi: (bi, 0, c_off + gi)),
            pl.BlockSpec((1, t, width), lambda bi, gi: (bi, 0, gi)),
            all_heads, all_heads, row_spec, row_spec,
            pl.BlockSpec((1, width), lambda bi, gi: (0, gi)),
            pl.BlockSpec((1, width), lambda bi, gi: (0, gi)),
        ],
        out_specs=pl.BlockSpec((1, t, width), lambda bi, gi: (bi, 0, gi)),
        out_shape=jax.ShapeDtypeStruct((b, t, inner), BF16),
        scratch_shapes=[pltpu.VMEM((t, width), F32), pltpu.VMEM((2, n, width), F32)],
        compiler_params=_params("parallel", "arbitrary"),
        name="ssd_scan",
    )(xbc, xbc, xbc, zx, acum, dt, arow, drow,
      d_skip_wide.reshape(1, inner).astype(F32), norm_g.reshape(1, inner).astype(F32))


def _even_mixer(xf, b, t, gn, w_in, q_gain, k_gain, w_out):
    d = xf.shape[1]
    proj = norm_matmul(xf, gn[0], w_in).reshape(b, t, -1)
    heads = jnp.arange(RET_HEADS, dtype=F32)
    lg = jnp.stack([jnp.log1p(-jnp.exp2(-(RET_DECAY_EXP_FWD + heads))),
                    jnp.log1p(-jnp.exp2(-(RET_DECAY_EXP_BWD + heads)))], axis=1)
    cos1, sin1 = _retention_tables(t)
    ret = retention(proj, lg, cos1, sin1)
    cos2, sin2 = _axial_tables(t)
    att_base = 2 * RET_HEADS * RET_DK + 2 * RET_HEADS * RET_DV
    gw = ATT_GROUP * ATT_HD
    k_base = att_base + ATT_HEADS * ATT_HD
    v_base = k_base + ATT_KV_HEADS * ATT_HD
    att = gqa_attention(proj, att_base // gw, k_base // ATT_HD, v_base // ATT_HD,
                        q_gain, k_gain, cos2, sin2)
    parts = [ret.reshape(b * t, -1), att.reshape(b * t, -1)]
    return matmul_norm_resid(parts, w_out, gn[1], xf)


def _odd_mixer(xf, b, t, gn, w_in, conv_w, conv_b, a_log, dt_bias, d_skip, norm_g, w_out):
    inner = norm_g.shape[0]
    nh = d_skip.shape[0]
    g = SSD_GROUPS
    kh = nh // g
    zx, dt_raw = norm_matmul(xf, gn[0], w_in, n_main=w_in.shape[1] - 2 * nh)
    zx = zx.reshape(b, t, -1)
    dt_raw = dt_raw.reshape(b, t, 2 * nh)
    xbc = conv_silu(zx, inner // 512, conv_w, conv_b)
    dt, acum = dt_prep(dt_raw, dt_bias.reshape(-1), a_log.reshape(-1))
    chunk = _pick_tile(t, SSD_CHUNK)
    rows_of = lambda arr: arr.reshape(b, t // chunk, chunk, 2, g, kh).transpose(0, 3, 4, 1, 5, 2)
    y = ssd_scan(xbc, zx, acum, dt, rows_of(acum), rows_of(dt),
                 jnp.repeat(d_skip.astype(F32), inner // nh), norm_g)
    return matmul_norm_resid([y.reshape(b * t, inner)], w_out, gn[1], xf)


def _trunk(x, mem, norm_g, wts):
    b, t, d = x.shape
    nm = mem.shape[1]
    xf = x.reshape(b * t, d)
    memf = mem.reshape(b * nm, d)
    depth = norm_g.shape[0]
    for i in range(depth):
        gn = norm_g[i]
        j = i // 2
        if i % 2 == 0:
            xf = _even_mixer(xf, b, t, gn, wts["ev_w_in"][j], wts["ev_q_gain"][j], wts["ev_k_gain"][j],
                             wts["ev_w_out"][j])
        else:
            xf = _odd_mixer(xf, b, t, gn, wts["od_w_in"][j], wts["od_conv_w"][j],
                            wts["od_conv_b"][j], wts["od_a_log"][j], wts["od_dt_bias"][j], wts["od_d"][j],
                            wts["od_norm_g"][j], wts["od_w_out"][j])
        q = norm_matmul(xf, gn[2], wts["xa_wq"][i]).reshape(b, t, d)
        kv = norm_matmul(memf, gn[4], wts["xa_wkv"][i]).reshape(b, nm, 2 * d)
        xa = cross_attention(q, kv).reshape(b * t, d)
        xf = matmul_norm_resid([xa], wts["xa_wo"][i], gn[3], xf)
        act = norm_swiglu(xf, gn[5], wts["ffn_w_gu"][i])
        xf = matmul_norm_resid([act], wts["ffn_w_down"][i], gn[6], xf)
    return xf.reshape(b, t, d)


def kernel(x_prompt, x_sample, mem_prompt, mem_sample, norm_g, ev_w_in, ev_q_gain, ev_k_gain, ev_w_out,
           od_w_in, od_conv_w, od_conv_b, od_a_log, od_dt_bias, od_d, od_norm_g, od_w_out,
           xa_wq, xa_wkv, xa_wo, ffn_w_gu, ffn_w_down):
    wts = dict(
        ev_w_in=ev_w_in.astype(BF16), ev_q_gain=ev_q_gain, ev_k_gain=ev_k_gain,
        ev_w_out=ev_w_out.astype(BF16),
        od_w_in=od_w_in.astype(BF16),
        od_conv_w=od_conv_w, od_conv_b=od_conv_b, od_a_log=od_a_log, od_dt_bias=od_dt_bias,
        od_d=od_d, od_norm_g=od_norm_g, od_w_out=od_w_out.astype(BF16),
        xa_wq=xa_wq.astype(BF16), xa_wkv=xa_wkv.astype(BF16), xa_wo=xa_wo.astype(BF16),
        ffn_w_gu=ffn_w_gu.astype(BF16), ffn_w_down=ffn_w_down.astype(BF16),
    )
    y_prompt = _trunk(x_prompt, mem_prompt, norm_g, wts)
    y_sample = _trunk(x_sample, mem_sample, norm_g, wts)
    return (y_prompt, y_sample)
```

```python
import functools

import jax
import jax.numpy as jnp
from jax import lax
from jax.experimental import pallas as pl
from jax.experimental.pallas import tpu as pltpu

F32 = jnp.float32
BF16 = jnp.bfloat16

EPS = 1e-6
ROPE_THETA = 10000.0
GRID_W = 64

RET_HEADS = 4
RET_DK = 256
RET_DV = 256
RET_DECAY_EXP_FWD = 5.0
RET_DECAY_EXP_BWD = 5.5
ATT_HEADS = 8
ATT_KV_HEADS = 2
ATT_HD = 128
ATT_GROUP = ATT_HEADS // ATT_KV_HEADS

SSD_HEADDIM = 64
SSD_STATE = 128
SSD_GROUPS = 8
SSD_CONV = 5
SSD_CHUNK = 128

XA_HEADS = 4

VMEM_LIMIT_BYTES = 56 * 1024 * 1024
LANES = 128
SUBLANES = 8

RET_CHUNK = 256
ATT_BLOCK_Q = 128
XA_BLOCK_Q = 512
CONV_ROWS = 256
NEG_BIG = -1e30
RESIDENT_WEIGHT_BYTES = 24 * 1024 * 1024
N_CHUNK = 512
ATT_KV_BLOCKS = 2
LOG2E = 1.4426950408889634


def _params(*sem):
    return pltpu.CompilerParams(dimension_semantics=sem, vmem_limit_bytes=VMEM_LIMIT_BYTES)


def _dot(a, b):
    return jnp.dot(a, b, preferred_element_type=F32)


def _dot_nt(a, b):
    return lax.dot_general(a, b, (((1,), (1,)), ((), ())), preferred_element_type=F32)


def _dot_tn(a, b):
    return lax.dot_general(a, b, (((0,), (0,)), ((), ())), preferred_element_type=F32)


def _silu(x):
    h = 0.5 * x
    return h + h * jnp.tanh(h)


def _rms(x, g):
    return x * lax.rsqrt(jnp.mean(x * x, axis=-1, keepdims=True) + EPS) * g


def _pick_tile(n, pref):
    t = min(pref, n)
    while n % t:
        t //= 2
    return t


def _resident(shape):
    zeros = (0,) * len(shape)
    return pl.BlockSpec(shape, lambda *_: zeros, pipeline_mode=pl.Buffered(1))


def _nbytes(a):
    return a.size * a.dtype.itemsize


def _norm_matmul_resident_kernel(x_ref, g_ref, w_ref, o_ref, h_ref, *, n_chunk):
    h_ref[...] = _rms(x_ref[...], g_ref[...]).astype(BF16)
    for c0 in range(0, w_ref.shape[1], n_chunk):
        cols = slice(c0, c0 + n_chunk)
        o_ref[:, cols] = _dot(h_ref[...], w_ref[:, cols]).astype(o_ref.dtype)


def norm_matmul(x, g, w):
    m, d = x.shape
    n = w.shape[1]
    assert _nbytes(w) <= RESIDENT_WEIGHT_BYTES
    tm = _pick_tile(m, 1024 if _nbytes(w) <= RESIDENT_WEIGHT_BYTES // 3 else 512)
    return pl.pallas_call(
        functools.partial(_norm_matmul_resident_kernel, n_chunk=_pick_tile(n, N_CHUNK)),
        grid=(m // tm,),
        in_specs=[pl.BlockSpec((tm, d), lambda i: (i, 0)), _resident((1, d)), _resident((d, n))],
        out_specs=pl.BlockSpec((tm, n), lambda i: (i, 0)),
        out_shape=jax.ShapeDtypeStruct((m, n), BF16),
        scratch_shapes=[pltpu.VMEM((tm, d), BF16)],
        compiler_params=_params("parallel"),
        name="norm_matmul_resident",
    )(x, g.reshape(1, d).astype(F32), w)


def _norm_swiglu_kernel(x_ref, g_ref, wg_ref, wu_ref, o_ref, h_ref):
    @pl.when(pl.program_id(1) == 0)
    def _():
        h_ref[...] = _rms(x_ref[...], g_ref[...]).astype(BF16)

    h = h_ref[...]
    gate = _dot(h, wg_ref[...])
    up = _dot(h, wu_ref[...])
    o_ref[...] = (_silu(gate) * up).astype(o_ref.dtype)


def norm_swiglu(x, g, w_gu, tm=1024, tn=512):
    m, d = x.shape
    f = w_gu.shape[1] // 2
    tm = _pick_tile(m, tm)
    tn = _pick_tile(f, tn)
    nj = f // tn
    return pl.pallas_call(
        _norm_swiglu_kernel,
        grid=(m // tm, nj),
        in_specs=[
            pl.BlockSpec((tm, d), lambda i, j: (i, 0)),
            pl.BlockSpec((1, d), lambda i, j: (0, 0)),
            pl.BlockSpec((d, tn), lambda i, j: (0, j)),
            pl.BlockSpec((d, tn), lambda i, j: (0, j + nj)),
        ],
        out_specs=pl.BlockSpec((tm, tn), lambda i, j: (i, j)),
        out_shape=jax.ShapeDtypeStruct((m, f), BF16),
        scratch_shapes=[pltpu.VMEM((tm, d), BF16)],
        compiler_params=_params("parallel", "arbitrary"),
        name="norm_swiglu",
    )(x, g.reshape(1, d).astype(F32), w_gu, w_gu)


def _matmul_norm_resid_kernel(*refs, n_a, n_chunk):
    a_refs = refs[:n_a]
    w_refs = refs[n_a:2 * n_a]
    g_ref, x_ref, o_ref = refs[2 * n_a:]
    tm, d = o_ref.shape
    ss = jnp.zeros((tm, 1), F32)
    for c0 in range(0, d, n_chunk):
        cols = slice(c0, c0 + n_chunk)
        y = _dot(a_refs[0][...], w_refs[0][:, cols])
        for a_ref, w_ref in zip(a_refs[1:], w_refs[1:]):
            y = y + _dot(a_ref[...], w_ref[:, cols])
        o_ref[:, cols] = y
        ss = ss + jnp.sum(y * y, axis=-1, keepdims=True)
    scale = lax.rsqrt(ss / d + EPS)
    for c0 in range(0, d, n_chunk):
        cols = slice(c0, c0 + n_chunk)
        o_ref[:, cols] = x_ref[:, cols] + o_ref[:, cols] * scale * g_ref[:, cols]


def matmul_norm_resid(parts, w, g, x, tm=512):
    m = parts[0].shape[0]
    d = w.shape[1]
    tm = _pick_tile(m, tm)
    n_a = len(parts)
    kp = parts[0].shape[1]
    assert all(p.shape[1] == kp for p in parts) and n_a * kp == w.shape[0]
    assert _nbytes(w) <= RESIDENT_WEIGHT_BYTES
    in_specs = [pl.BlockSpec((tm, kp), lambda i: (i, 0)) for _ in parts]
    in_specs += [pl.BlockSpec((kp, d), lambda i, r=r: (r, 0), pipeline_mode=pl.Buffered(1)) for r in range(n_a)]
    in_specs += [_resident((1, d)), pl.BlockSpec((tm, d), lambda i: (i, 0))]
    return pl.pallas_call(
        functools.partial(_matmul_norm_resid_kernel, n_a=n_a, n_chunk=_pick_tile(d, N_CHUNK)),
        grid=(m // tm,),
        in_specs=in_specs,
        out_specs=pl.BlockSpec((tm, d), lambda i: (i, 0)),
        out_shape=jax.ShapeDtypeStruct((m, d), F32),
        compiler_params=_params("parallel"),
        name="matmul_norm_resid",
    )(*parts, *([w] * n_a), g.reshape(1, d).astype(F32), x)


def _rope_angles(pos, dim):
    inv = ROPE_THETA ** (-jnp.arange(0, dim, 2, dtype=F32) / dim)
    ang = pos.astype(F32)[:, None] * inv[None, :]
    return jnp.cos(ang), jnp.sin(ang)


def _retention_tables(t):
    return _rope_angles(jnp.arange(t), RET_DK)


def _axial_tables(t):
    rows = t // GRID_W
    row = jnp.repeat(jnp.arange(rows), GRID_W)
    col = jnp.tile(jnp.arange(GRID_W), rows)
    half = ATT_HD // 2
    cos_r, sin_r = _rope_angles(row, half)
    cos_c, sin_c = _rope_angles(col, half)
    cos = jnp.concatenate([cos_r, cos_r, cos_c, cos_c], axis=-1)
    sin = jnp.concatenate([-sin_r, sin_r, -sin_c, sin_c], axis=-1)
    return cos, sin


def _retention_kernel(lg_ref, q_ref, k_ref, v_ref, g_ref, cos_ref, sin_ref, o_ref,
                      qs_ref, ks_ref, acc_ref, st_ref, *, chunk):
    t = q_ref.shape[1]
    dk = q_ref.shape[2]
    hk = dk // 2
    nc = t // chunk
    h = pl.program_id(1)
    lg_f = lg_ref[h, 0]
    lg_b = lg_ref[h, 1]

    def rope(x, cos, sin):
        x1 = x[:, :hk]
        x2 = x[:, hk:]
        return jnp.concatenate([x1 * cos - x2 * sin, x2 * cos + x1 * sin], axis=-1)

    def prep(c, carry):
        rows = pl.ds(pl.multiple_of(c * chunk, chunk), chunk)
        qs_ref[rows, :] = rope(q_ref[0, rows, :].astype(F32), cos_ref[rows, :], sin_ref[rows, :])
        ks_ref[rows, :] = rope(k_ref[0, rows, :].astype(F32), cos_ref[rows, :], sin_ref[rows, :]) * (dk ** -0.5)
        return carry

    lax.fori_loop(0, nc, prep, 0)

    idx = lax.broadcasted_iota(jnp.int32, (chunk, 1), 0).astype(F32)
    ri = lax.broadcasted_iota(jnp.int32, (chunk, chunk), 0)
    ci = lax.broadcasted_iota(jnp.int32, (chunk, chunk), 1)
    diff = (ri - ci).astype(F32)
    d_f = jnp.where(ri >= ci, jnp.exp(lg_f * jnp.where(ri >= ci, diff, 0.0)), 0.0)
    d_b = jnp.where(ri < ci, jnp.exp(lg_b * jnp.where(ri < ci, -diff, 0.0)), 0.0)
    xi_f = jnp.exp(lg_f * (idx + 1.0))
    zeta_f = jnp.exp(lg_f * (chunk - 1.0 - idx))
    xi_b = jnp.exp(lg_b * (chunk - idx))
    zeta_b = jnp.exp(lg_b * idx)
    one = jnp.ones((1, 1), F32)
    gc_f = jnp.exp(one * (lg_f * chunk))
    gc_b = jnp.exp(one * (lg_b * chunk))

    def scores(rows):
        qc = qs_ref[rows, :]
        kc = ks_ref[rows, :]
        kb = kc.astype(BF16)
        return qc, kb, _dot_nt(qc.astype(BF16), kb)

    def step(d, rows, dmat, xi, zeta, gc):
        qc, kb, s = scores(rows)
        vc = v_ref[0, rows, :]
        inner = _dot((s * dmat).astype(BF16), vc)
        cross = _dot((qc * xi).astype(BF16), st_ref[d].astype(BF16))
        st_ref[d] = gc * st_ref[d] + _dot_tn(kb, (vc.astype(F32) * zeta).astype(BF16))
        acc_ref[rows, :] += inner + cross

    acc_ref[...] = jnp.zeros_like(acc_ref)
    st_ref[...] = jnp.zeros_like(st_ref)

    def scan(i, carry):
        step(0, pl.ds(pl.multiple_of(i * chunk, chunk), chunk), d_f, xi_f, zeta_f, gc_f)
        step(1, pl.ds(pl.multiple_of((nc - 1 - i) * chunk, chunk), chunk), d_b, xi_b, zeta_b, gc_b)
        return carry

    lax.fori_loop(0, nc, scan, 0, unroll=2)

    def finish(c, carry):
        rows = pl.ds(pl.multiple_of(c * chunk, chunk), chunk)
        ret = acc_ref[rows, :]
        mu = jnp.mean(ret, axis=-1, keepdims=True)
        cen = ret - mu
        var = jnp.mean(cen * cen, axis=-1, keepdims=True)
        gate = _silu(g_ref[0, rows, :].astype(F32))
        o_ref[0, rows, :] = (gate * (cen * lax.rsqrt(var + EPS))).astype(o_ref.dtype)
        return carry

    lax.fori_loop(0, nc, finish, 0)


def retention(proj, lg, cos, sin):
    b, t, _ = proj.shape
    hh = RET_HEADS
    chunk = _pick_tile(t, RET_CHUNK)
    blk = lambda off: pl.BlockSpec((1, t, RET_DK), lambda bi, hi: (bi, 0, off + hi))
    return pl.pallas_call(
        functools.partial(_retention_kernel, chunk=chunk),
        grid=(b, hh),
        in_specs=[
            pl.BlockSpec(memory_space=pltpu.SMEM),
            blk(0), blk(hh), blk(2 * hh), blk(3 * hh),
            pl.BlockSpec((t, RET_DK // 2), lambda bi, hi: (0, 0)),
            pl.BlockSpec((t, RET_DK // 2), lambda bi, hi: (0, 0)),
        ],
        out_specs=pl.BlockSpec((1, t, RET_DV), lambda bi, hi: (bi, 0, hi)),
        out_shape=jax.ShapeDtypeStruct((b, t, hh * RET_DV), BF16),
        scratch_shapes=[
            pltpu.VMEM((t, RET_DK), F32),
            pltpu.VMEM((t, RET_DK), F32),
            pltpu.VMEM((t, RET_DV), F32),
            pltpu.VMEM((2, RET_DK, RET_DV), F32),
        ],
        compiler_params=_params("parallel", "parallel"),
        name="retention",
    )(lg, proj, proj, proj, proj, cos, sin)


def _axial_rope(x, cos, sin_signed):
    quarter = x.shape[-1] // 4
    lane = lax.broadcasted_iota(jnp.int32, x.shape, 1)
    first = (lane % (2 * quarter)) < quarter
    partner = jnp.where(first,
                        pltpu.roll(x, x.shape[-1] - quarter, axis=1),
                        pltpu.roll(x, quarter, axis=1))
    return x * cos + partner * sin_signed


def _gqa_kernel(q_ref, k_ref, v_ref, qg_ref, kg_ref, cosq_ref, sinq_ref, cosk_ref, sink_ref,
                o_ref, kt_ref):
    bq = q_ref.shape[1]
    hd = k_ref.shape[2]
    grp = q_ref.shape[2] // hd

    @pl.when(pl.program_id(2) == 0)
    def _():
        kn = _rms(k_ref[0].astype(F32), kg_ref[...])
        kt_ref[...] = _axial_rope(kn, cosk_ref[...], sink_ref[...]).astype(BF16)

    cos = cosq_ref[...]
    sin = sinq_ref[...]
    qs = []
    for h in range(grp):
        qn = _rms(q_ref[0, :, h * hd:(h + 1) * hd].astype(F32), qg_ref[...])
        qs.append((_axial_rope(qn, cos, sin) * (hd ** -0.5 * LOG2E)).astype(BF16))
    q = jnp.concatenate(qs, axis=0)
    t = kt_ref.shape[0]
    kb = t // _pick_tile(t // LANES, ATT_KV_BLOCKS)
    blocks = [slice(j, j + kb) for j in range(0, t, kb)]
    m = l = acc = None
    for blk in blocks:
        s = _dot_nt(q, kt_ref[blk, :])
        m_blk = jnp.max(s, axis=-1, keepdims=True)
        if m is None:
            m = m_blk
            p = jnp.exp2(s - m)
            l = jnp.sum(p, axis=-1, keepdims=True)
            acc = _dot(p.astype(BF16), v_ref[0, blk, :])
        else:
            m_new = jnp.maximum(m, m_blk)
            alpha = jnp.exp2(m - m_new)
            p = jnp.exp2(s - m_new)
            l = l * alpha + jnp.sum(p, axis=-1, keepdims=True)
            acc = acc * alpha + _dot(p.astype(BF16), v_ref[0, blk, :])
            m = m_new
    o = acc / l
    for h in range(grp):
        o_ref[0, :, h * hd:(h + 1) * hd] = o[h * bq:(h + 1) * bq].astype(o_ref.dtype)


def gqa_attention(proj, q_off, k_off, v_off, q_gain, k_gain, cos, sin):
    b, t, _ = proj.shape
    hd = ATT_HD
    gw = ATT_GROUP * hd
    bq = _pick_tile(t, ATT_BLOCK_Q)
    return pl.pallas_call(
        _gqa_kernel,
        grid=(b, ATT_KV_HEADS, t // bq),
        in_specs=[
            pl.BlockSpec((1, bq, gw), lambda bi, ki, qi: (bi, qi, q_off + ki)),
            pl.BlockSpec((1, t, hd), lambda bi, ki, qi: (bi, 0, k_off + ki)),
            pl.BlockSpec((1, t, hd), lambda bi, ki, qi: (bi, 0, v_off + ki)),
            pl.BlockSpec((1, hd), lambda bi, ki, qi: (0, 0)),
            pl.BlockSpec((1, hd), lambda bi, ki, qi: (0, 0)),
            pl.BlockSpec((bq, hd), lambda bi, ki, qi: (qi, 0)),
            pl.BlockSpec((bq, hd), lambda bi, ki, qi: (qi, 0)),
            pl.BlockSpec((t, hd), lambda bi, ki, qi: (0, 0)),
            pl.BlockSpec((t, hd), lambda bi, ki, qi: (0, 0)),
        ],
        out_specs=pl.BlockSpec((1, bq, gw), lambda bi, ki, qi: (bi, qi, ki)),
        out_shape=jax.ShapeDtypeStruct((b, t, ATT_HEADS * hd), BF16),
        scratch_shapes=[pltpu.VMEM((t, hd), BF16)],
        compiler_params=_params("parallel", "parallel", "arbitrary"),
        name="gqa_attention",
    )(proj, proj, proj, q_gain.reshape(1, hd).astype(F32), k_gain.reshape(1, hd).astype(F32),
      cos, sin, cos, sin)


def _cross_attn_kernel(q_ref, kv_ref, o_ref, *, heads):
    d = q_ref.shape[2]
    hd = d // heads
    for h in range(heads):
        qh = q_ref[0, :, h * hd:(h + 1) * hd]
        kh = kv_ref[0, :, h * hd:(h + 1) * hd]
        vh = kv_ref[0, :, d + h * hd:d + (h + 1) * hd]
        s = _dot_nt(qh, kh) * (hd ** -0.5)
        m = jnp.max(s, axis=-1, keepdims=True)
        p = jnp.exp(s - m)
        l = jnp.sum(p, axis=-1, keepdims=True)
        o_ref[0, :, h * hd:(h + 1) * hd] = (_dot(p.astype(BF16), vh) / l).astype(o_ref.dtype)


def cross_attention(q, kv):
    b, t, d = q.shape
    nm = kv.shape[1]
    tq = _pick_tile(t, XA_BLOCK_Q)
    return pl.pallas_call(
        functools.partial(_cross_attn_kernel, heads=XA_HEADS),
        grid=(b, t // tq),
        in_specs=[
            pl.BlockSpec((1, tq, d), lambda bi, qi: (bi, qi, 0)),
            pl.BlockSpec((1, nm, 2 * d), lambda bi, qi: (bi, 0, 0)),
        ],
        out_specs=pl.BlockSpec((1, tq, d), lambda bi, qi: (bi, qi, 0)),
        out_shape=jax.ShapeDtypeStruct((b, t, d), BF16),
        compiler_params=_params("parallel", "parallel"),
        name="cross_attention",
    )(q, kv)


def _in_proj_conv_kernel(x_ref, g_ref, w_ref, w2_ref, cw_ref, cb_ref, o_ref, o2_ref, h_ref, pad_ref,
                         *, z_tiles, n_sub, rows):
    j = pl.program_id(1)
    t = x_ref.shape[0]
    tn = o_ref.shape[1]
    sub = tn // n_sub
    halo = SUBLANES
    taps = cw_ref.shape[0]

    @pl.when(j == 0)
    def _():
        def norm_rows(i, carry):
            r = pl.ds(pl.multiple_of(i * rows, rows), rows)
            h_ref[r, :] = _rms(x_ref[r, :], g_ref[...]).astype(BF16)
            return carry

        lax.fori_loop(0, t // rows, norm_rows, 0)
        o2_ref[...] = _dot(h_ref[...], w2_ref[...])
        zeros = jnp.zeros((halo, sub), F32)
        for s in range(pad_ref.shape[0]):
            pad_ref[s, 0:halo, :] = zeros
            pad_ref[s, halo + t:2 * halo + t, :] = zeros

    @pl.when(j < z_tiles)
    def _():
        for c in range(n_sub):
            cols = slice(c * sub, (c + 1) * sub)
            o_ref[:, cols] = _dot(h_ref[...], w_ref[:, cols]).astype(o_ref.dtype)

    @pl.when(j >= z_tiles)
    def _():
        for c in range(n_sub):
            cols = slice(c * sub, (c + 1) * sub)
            slot = c % pad_ref.shape[0]
            pad_ref[slot, halo:halo + t, :] = _dot(h_ref[...], w_ref[:, cols])
            w = cw_ref[:, cols]
            bias = cb_ref[:, cols]
            for r0 in range(0, t, rows):
                win = pad_ref[slot, r0:r0 + rows + 2 * halo, :]
                acc = jnp.zeros((rows, sub), F32)
                for k in range(taps):
                    shift = (taps // 2 - k) % (rows + 2 * halo)
                    shifted = win if shift == 0 else pltpu.roll(win, shift, axis=0)
                    acc = acc + shifted[halo:halo + rows, :] * w[k:k + 1, :]
                o_ref[r0:r0 + rows, cols] = _silu(acc + bias).astype(o_ref.dtype)


def in_proj_conv(x, g, w, conv_w, conv_b, t, n_z, n_extra, tn=1024, n_sub=4):
    m, d = x.shape
    n = w.shape[1] - n_extra
    taps, c = conv_w.shape
    assert n == n_z + c and n_z % tn == 0 and c % tn == 0 and n % n_extra == 0 and m % t == 0
    z_tiles = n_z // tn
    rows = _pick_tile(t, CONV_ROWS)
    conv_col = lambda i, j: (0, jnp.maximum(j - z_tiles, 0))
    return pl.pallas_call(
        functools.partial(_in_proj_conv_kernel, z_tiles=z_tiles, n_sub=n_sub, rows=rows),
        grid=(m // t, n // tn),
        in_specs=[
            pl.BlockSpec((t, d), lambda i, j: (i, 0), pipeline_mode=pl.Buffered(1)),
            _resident((1, d)),
            pl.BlockSpec((d, tn), lambda i, j: (0, j)),
            pl.BlockSpec((d, n_extra), lambda i, j: (0, n // n_extra), pipeline_mode=pl.Buffered(1)),
            pl.BlockSpec((taps, tn), conv_col),
            pl.BlockSpec((1, tn), conv_col),
        ],
        out_specs=[pl.BlockSpec((t, tn), lambda i, j: (i, j)), pl.BlockSpec((t, n_extra), lambda i, j: (i, 0))],
        out_shape=[jax.ShapeDtypeStruct((m, n), BF16), jax.ShapeDtypeStruct((m, n_extra), F32)],
        scratch_shapes=[pltpu.VMEM((t, d), BF16), pltpu.VMEM((2, t + 2 * SUBLANES, tn // n_sub), F32)],
        compiler_params=_params("parallel", "arbitrary"),
        name="in_proj_conv",
    )(x, g.reshape(1, d).astype(F32), w, w, conv_w.astype(F32), conv_b.reshape(1, c).astype(F32))


def _dt_prep_kernel(raw_ref, bias_ref, alog_ref, dt_ref, acum_ref, *, chunk):
    t = raw_ref.shape[1]
    nh2 = raw_ref.shape[2]
    ri = lax.broadcasted_iota(jnp.int32, (chunk, chunk), 0)
    ci = lax.broadcasted_iota(jnp.int32, (chunk, chunk), 1)
    tri_f = (ci <= ri).astype(F32)
    tri_b = (ci >= ri).astype(F32)
    lane = lax.broadcasted_iota(jnp.int32, (chunk, nh2), 1)
    a = -jnp.exp(alog_ref[...])
    bias = bias_ref[...]

    def body(c, carry):
        rows = pl.ds(pl.multiple_of(c * chunk, chunk), chunk)
        dt = jax.nn.softplus(raw_ref[0, rows, :] + bias)
        dt_ref[0, rows, :] = dt
        da = dt * a
        pre = jnp.dot(tri_f, da, preferred_element_type=F32, precision=lax.Precision.HIGHEST)
        suf = jnp.dot(tri_b, da, preferred_element_type=F32, precision=lax.Precision.HIGHEST)
        acum_ref[0, rows, :] = jnp.where(lane < nh2 // 2, pre, suf)
        return carry

    lax.fori_loop(0, t // chunk, body, 0)


def dt_prep(dt_raw, dt_bias, a_log):
    b, t, nh2 = dt_raw.shape
    chunk = _pick_tile(t, SSD_CHUNK)
    spec = pl.BlockSpec((1, t, nh2), lambda bi: (bi, 0, 0))
    vec = pl.BlockSpec((1, nh2), lambda bi: (0, 0))
    return pl.pallas_call(
        functools.partial(_dt_prep_kernel, chunk=chunk),
        grid=(b,),
        in_specs=[spec, vec, vec],
        out_specs=[spec, spec],
        out_shape=[jax.ShapeDtypeStruct((b, t, nh2), F32)] * 2,
        compiler_params=_params("parallel"),
        name="ssd_dt_prep",
    )(dt_raw, dt_bias.reshape(1, nh2).astype(F32), a_log.reshape(1, nh2).astype(F32))


def _ssd_kernel(x_ref, b_ref, c_ref, z_ref, acum_ref, dt_ref, arow_ref, drow_ref, dskip_ref, ng_ref,
                o_ref, yacc_ref, st_ref, *, chunk):
    t = x_ref.shape[1]
    width = x_ref.shape[2]
    npair = width // LANES
    hp = LANES // 2
    heads = 2 * npair
    nc = t // chunk
    half_lanes = acum_ref.shape[2] // 2
    grp = pl.program_id(1)
    ri = lax.broadcasted_iota(jnp.int32, (chunk, chunk), 0)
    ci = lax.broadcasted_iota(jnp.int32, (chunk, chunk), 1)
    lane = lax.broadcasted_iota(jnp.int32, (chunk, LANES), 1)
    lo = lane < hp
    hi_i = (lane >= hp).astype(jnp.int32)

    def lane_take(a, idx):
        return jnp.take_along_axis(a, idx, axis=1, mode="promise_in_bounds")

    def dir_chunk(d, c):
        mask = (ri >= ci) if d == 0 else (ri <= ci)
        rows = pl.ds(pl.multiple_of(c * chunk, chunk), chunk)
        x = x_ref[0, rows, :].astype(F32)
        bc = b_ref[0, rows, :]
        cc = c_ref[0, rows, :]
        a_all = acum_ref[0, rows, :] * LOG2E
        dt_all = dt_ref[0, rows, :]
        a_end = a_all[chunk - 1:chunk, :] if d == 0 else a_all[0:1, :]
        w_all = jnp.exp2(a_end - a_all) * dt_all
        ee_all = jnp.broadcast_to(jnp.exp2(a_end), (SUBLANES, a_all.shape[1]))
        arow = arow_ref[0, d, 0, c] * LOG2E - jnp.log2(drow_ref[0, d, 0, c])
        base = d * half_lanes + grp * heads

        cb = _dot_nt(cc, bc)
        coff = _dot(cc, st_ref[d].astype(BF16))

        ys, xws, ends = [], [], []
        for p in range(npair):
            pair_idx = base + 2 * p + hi_i
            xt = x[:, p * LANES:(p + 1) * LANES]
            a_cols = [lane_take(a_all, jnp.zeros_like(lane) + (base + 2 * p + half)) for half in range(2)]
            y = coff[:, p * LANES:(p + 1) * LANES] * jnp.exp2(jnp.where(lo, a_cols[0], a_cols[1]))
            for half in range(2):
                k = 2 * p + half
                seg = a_cols[half] - arow[k:k + 1, :]
                mk = (cb * jnp.exp2(jnp.where(mask, seg, NEG_BIG))).astype(BF16)
                keep = lo if half == 0 else jnp.logical_not(lo)
                y = y + _dot(mk, jnp.where(keep, xt, 0.0).astype(BF16))
            ys.append(y)
            xws.append((xt * lane_take(w_all, pair_idx)).astype(BF16))
            ends.append(lane_take(ee_all, pair_idx[:SUBLANES])[0:1])
        y = jnp.concatenate(ys, axis=-1)
        xw = jnp.concatenate(xws, axis=-1)
        e_row = jnp.concatenate(ends, axis=-1)
        st_ref[d] = st_ref[d] * e_row + _dot_tn(bc, xw)
        yacc_ref[rows, :] += y

    yacc_ref[...] = jnp.zeros_like(yacc_ref)
    st_ref[...] = jnp.zeros_like(st_ref)

    def scan(i, carry):
        dir_chunk(0, i)
        dir_chunk(1, nc - 1 - i)
        return carry

    lax.fori_loop(0, nc, scan, 0, unroll=2)

    def finish(c, carry):
        rows = pl.ds(pl.multiple_of(c * chunk, chunk), chunk)
        tot = yacc_ref[rows, :] + dskip_ref[...] * x_ref[0, rows, :].astype(F32)
        tot = tot * _silu(z_ref[0, rows, :].astype(F32))
        o_ref[0, rows, :] = _rms(tot, ng_ref[...]).astype(o_ref.dtype)
        return carry

    lax.fori_loop(0, nc, finish, 0)


def ssd_scan(xbc, x_col, zx, acum, dt, arow, drow, d_skip_wide, norm_g):
    b, t, _ = xbc.shape
    g = SSD_GROUPS
    n = SSD_STATE
    inner = d_skip_wide.shape[0]
    width = inner // g
    kh = width // SSD_HEADDIM
    chunk = _pick_tile(t, SSD_CHUNK)
    x_off = x_col // width
    b_off = (x_col + inner) // n
    c_off = b_off + g
    nh2 = acum.shape[2]
    all_heads = pl.BlockSpec((1, t, nh2), lambda bi, gi: (bi, 0, 0))
    row_spec = pl.BlockSpec((1, 2, 1, t // chunk, kh, chunk), lambda bi, gi: (bi, 0, gi, 0, 0, 0))
    return pl.pallas_call(
        functools.partial(_ssd_kernel, chunk=chunk),
        grid=(b, g),
        in_specs=[
            pl.BlockSpec((1, t, width), lambda bi, gi: (bi, 0, x_off + gi)),
            pl.BlockSpec((1, t, n), lambda bi, gi: (bi, 0, b_off + gi)),
            pl.BlockSpec((1, t, n), lambda bi, gi: (bi, 0, c_off + gi)),
            pl.BlockSpec((1, t, width), lambda bi, gi: (bi, 0, gi)),
            all_heads, all_heads, row_spec, row_spec,
            pl.BlockSpec((1, width), lambda bi, gi: (0, gi)),
            pl.BlockSpec((1, width), lambda bi, gi: (0, gi)),
        ],
        out_specs=pl.BlockSpec((1, t, width), lambda bi, gi: (bi, 0, gi)),
        out_shape=jax.ShapeDtypeStruct((b, t, inner), BF16),
        scratch_shapes=[pltpu.VMEM((t, width), F32), pltpu.VMEM((2, n, width), F32)],
        compiler_params=_params("parallel", "arbitrary"),
        name="ssd_scan",
    )(xbc, xbc, xbc, zx, acum, dt, arow, drow,
      d_skip_wide.reshape(1, inner).astype(F32), norm_g.reshape(1, inner).astype(F32))


def _even_mixer(xf, b, t, gn, w_in, q_gain, k_gain, w_out):
    d = xf.shape[1]
    proj = norm_matmul(xf, gn[0], w_in).reshape(b, t, -1)
    heads = jnp.arange(RET_HEADS, dtype=F32)
    lg = jnp.stack([jnp.log1p(-jnp.exp2(-(RET_DECAY_EXP_FWD + heads))),
                    jnp.log1p(-jnp.exp2(-(RET_DECAY_EXP_BWD + heads)))], axis=1)
    cos1, sin1 = _retention_tables(t)
    ret = retention(proj, lg, cos1, sin1)
    cos2, sin2 = _axial_tables(t)
    att_base = 2 * RET_HEADS * RET_DK + 2 * RET_HEADS * RET_DV
    gw = ATT_GROUP * ATT_HD
    k_base = att_base + ATT_HEADS * ATT_HD
    v_base = k_base + ATT_KV_HEADS * ATT_HD
    att = gqa_attention(proj, att_base // gw, k_base // ATT_HD, v_base // ATT_HD,
                        q_gain, k_gain, cos2, sin2)
    parts = [ret.reshape(b * t, -1), att.reshape(b * t, -1)]
    return matmul_norm_resid(parts, w_out, gn[1], xf)


def _odd_mixer(xf, b, t, gn, w_in, conv_w, conv_b, a_log, dt_bias, d_skip, norm_g, w_out):
    inner = norm_g.shape[0]
    nh = d_skip.shape[0]
    g = SSD_GROUPS
    kh = nh // g
    zx, dt_raw = in_proj_conv(xf, gn[0], w_in, conv_w, conv_b, t, inner, 2 * nh)
    zx = zx.reshape(b, t, -1)
    dt_raw = dt_raw.reshape(b, t, 2 * nh)
    dt, acum = dt_prep(dt_raw, dt_bias.reshape(-1), a_log.reshape(-1))
    chunk = _pick_tile(t, SSD_CHUNK)
    rows_of = lambda arr: arr.reshape(b, t // chunk, chunk, 2, g, kh).transpose(0, 3, 4, 1, 5, 2)
    y = ssd_scan(zx, inner, zx, acum, dt, rows_of(acum), rows_of(dt),
                 jnp.repeat(d_skip.astype(F32), inner // nh), norm_g)
    return matmul_norm_resid([y.reshape(b * t, inner)], w_out, gn[1], xf)


def _trunk(x, mem, norm_g, wts):
    b, t, d = x.shape
    nm = mem.shape[1]
    xf = x.reshape(b * t, d)
    memf = mem.reshape(b * nm, d)
    depth = norm_g.shape[0]
    for i in range(depth):
        gn = norm_g[i]
        j = i // 2
        if i % 2 == 0:
            xf = _even_mixer(xf, b, t, gn, wts["ev_w_in"][j], wts["ev_q_gain"][j], wts["ev_k_gain"][j],
                             wts["ev_w_out"][j])
        else:
            xf = _odd_mixer(xf, b, t, gn, wts["od_w_in"][j], wts["od_conv_w"][j],
                            wts["od_conv_b"][j], wts["od_a_log"][j], wts["od_dt_bias"][j], wts["od_d"][j],
                            wts["od_norm_g"][j], wts["od_w_out"][j])
        q = norm_matmul(xf, gn[2], wts["xa_wq"][i]).reshape(b, t, d)
        kv = norm_matmul(memf, gn[4], wts["xa_wkv"][i]).reshape(b, nm, 2 * d)
        xa = cross_attention(q, kv).reshape(b * t, d)
        xf = matmul_norm_resid([xa], wts["xa_wo"][i], gn[3], xf)
        act = norm_swiglu(xf, gn[5], wts["ffn_w_gu"][i])
        xf = matmul_norm_resid([act], wts["ffn_w_down"][i], gn[6], xf)
    return xf.reshape(b, t, d)


def kernel(x_prompt, x_sample, mem_prompt, mem_sample, norm_g, ev_w_in, ev_q_gain, ev_k_gain, ev_w_out,
           od_w_in, od_conv_w, od_conv_b, od_a_log, od_dt_bias, od_d, od_norm_g, od_w_out,
           xa_wq, xa_wkv, xa_wo, ffn_w_gu, ffn_w_down):
    wts = dict(
        ev_w_in=ev_w_in.astype(BF16), ev_q_gain=ev_q_gain, ev_k_gain=ev_k_gain,
        ev_w_out=ev_w_out.astype(BF16),
        od_w_in=od_w_in.astype(BF16),
        od_conv_w=od_conv_w, od_conv_b=od_conv_b, od_a_log=od_a_log, od_dt_bias=od_dt_bias,
        od_d=od_d, od_norm_g=od_norm_g, od_w_out=od_w_out.astype(BF16),
        xa_wq=xa_wq.astype(BF16), xa_wkv=xa_wkv.astype(BF16), xa_wo=xa_wo.astype(BF16),
        ffn_w_gu=ffn_w_gu.astype(BF16), ffn_w_down=ffn_w_down.astype(BF16),
    )
    y_prompt = _trunk(x_prompt, mem_prompt, norm_g, wts)
    y_sample = _trunk(x_sample, mem_sample, norm_g, wts)
    return (y_prompt, y_sample)
```

```python
import functools

import jax
import jax.numpy as jnp
from jax import lax
from jax.experimental import pallas as pl
from jax.experimental.pallas import tpu as pltpu

F32 = jnp.float32
BF16 = jnp.bfloat16

EPS = 1e-6
ROPE_THETA = 10000.0
GRID_W = 64

RET_HEADS = 4
RET_DK = 256
RET_DV = 256
RET_DECAY_EXP_FWD = 5.0
RET_DECAY_EXP_BWD = 5.5
ATT_HEADS = 8
ATT_KV_HEADS = 2
ATT_HD = 128
ATT_GROUP = ATT_HEADS // ATT_KV_HEADS

SSD_HEADDIM = 64
SSD_STATE = 128
SSD_GROUPS = 8
SSD_CONV = 5
SSD_CHUNK = 128

XA_HEADS = 4

VMEM_LIMIT_BYTES = 56 * 1024 * 1024
LANES = 128
SUBLANES = 8

RET_CHUNK = 256
ATT_BLOCK_Q = 128
XA_BLOCK_Q = 512
CONV_ROWS = 256
NEG_BIG = -1e30
RESIDENT_WEIGHT_BYTES = 24 * 1024 * 1024
N_CHUNK = 512
ATT_KV_BLOCKS = 2
LOG2E = 1.4426950408889634


def _params(*sem):
    return pltpu.CompilerParams(dimension_semantics=sem, vmem_limit_bytes=VMEM_LIMIT_BYTES)


def _dot(a, b):
    return jnp.dot(a, b, preferred_element_type=F32)


def _dot_nt(a, b):
    return lax.dot_general(a, b, (((1,), (1,)), ((), ())), preferred_element_type=F32)


def _dot_tn(a, b):
    return lax.dot_general(a, b, (((0,), (0,)), ((), ())), preferred_element_type=F32)


def _silu(x):
    h = 0.5 * x
    return h + h * jnp.tanh(h)


def _rms(x, g):
    return x * lax.rsqrt(jnp.mean(x * x, axis=-1, keepdims=True) + EPS) * g


def _pick_tile(n, pref):
    t = min(pref, n)
    while n % t:
        t //= 2
    return t


def _resident(shape):
    zeros = (0,) * len(shape)
    return pl.BlockSpec(shape, lambda *_: zeros, pipeline_mode=pl.Buffered(1))


def _nbytes(a):
    return a.size * a.dtype.itemsize


def _norm_matmul_resident_kernel(x_ref, g_ref, w_ref, o_ref, h_ref, *, n_chunk):
    h_ref[...] = _rms(x_ref[...], g_ref[...]).astype(BF16)
    for c0 in range(0, w_ref.shape[1], n_chunk):
        cols = slice(c0, c0 + n_chunk)
        o_ref[:, cols] = _dot(h_ref[...], w_ref[:, cols]).astype(o_ref.dtype)


def norm_matmul(x, g, w):
    m, d = x.shape
    n = w.shape[1]
    assert _nbytes(w) <= RESIDENT_WEIGHT_BYTES
    tm = _pick_tile(m, 1024 if _nbytes(w) <= RESIDENT_WEIGHT_BYTES // 3 else 512)
    return pl.pallas_call(
        functools.partial(_norm_matmul_resident_kernel, n_chunk=_pick_tile(n, N_CHUNK)),
        grid=(m // tm,),
        in_specs=[pl.BlockSpec((tm, d), lambda i: (i, 0)), _resident((1, d)), _resident((d, n))],
        out_specs=pl.BlockSpec((tm, n), lambda i: (i, 0)),
        out_shape=jax.ShapeDtypeStruct((m, n), BF16),
        scratch_shapes=[pltpu.VMEM((tm, d), BF16)],
        compiler_params=_params("parallel"),
        name="norm_matmul_resident",
    )(x, g.reshape(1, d).astype(F32), w)


def _norm_swiglu_kernel(x_ref, g_ref, wg_ref, wu_ref, o_ref, h_ref):
    @pl.when(pl.program_id(1) == 0)
    def _():
        h_ref[...] = _rms(x_ref[...], g_ref[...]).astype(BF16)

    h = h_ref[...]
    gate = _dot(h, wg_ref[...])
    up = _dot(h, wu_ref[...])
    o_ref[...] = (_silu(gate) * up).astype(o_ref.dtype)


def norm_swiglu(x, g, w_gu, tm=1024, tn=512):
    m, d = x.shape
    f = w_gu.shape[1] // 2
    tm = _pick_tile(m, tm)
    tn = _pick_tile(f, tn)
    nj = f // tn
    return pl.pallas_call(
        _norm_swiglu_kernel,
        grid=(m // tm, nj),
        in_specs=[
            pl.BlockSpec((tm, d), lambda i, j: (i, 0)),
            pl.BlockSpec((1, d), lambda i, j: (0, 0)),
            pl.BlockSpec((d, tn), lambda i, j: (0, j)),
            pl.BlockSpec((d, tn), lambda i, j: (0, j + nj)),
        ],
        out_specs=pl.BlockSpec((tm, tn), lambda i, j: (i, j)),
        out_shape=jax.ShapeDtypeStruct((m, f), BF16),
        scratch_shapes=[pltpu.VMEM((tm, d), BF16)],
        compiler_params=_params("parallel", "arbitrary"),
        name="norm_swiglu",
    )(x, g.reshape(1, d).astype(F32), w_gu, w_gu)


def _matmul_norm_resid_kernel(*refs, n_a, n_chunk):
    a_refs = refs[:n_a]
    w_refs = refs[n_a:2 * n_a]
    g_ref, x_ref, o_ref = refs[2 * n_a:]
    tm, d = o_ref.shape
    ss = jnp.zeros((tm, 1), F32)
    for c0 in range(0, d, n_chunk):
        cols = slice(c0, c0 + n_chunk)
        y = _dot(a_refs[0][...], w_refs[0][:, cols])
        for a_ref, w_ref in zip(a_refs[1:], w_refs[1:]):
            y = y + _dot(a_ref[...], w_ref[:, cols])
        o_ref[:, cols] = y
        ss = ss + jnp.sum(y * y, axis=-1, keepdims=True)
    scale = lax.rsqrt(ss / d + EPS)
    for c0 in range(0, d, n_chunk):
        cols = slice(c0, c0 + n_chunk)
        o_ref[:, cols] = x_ref[:, cols] + o_ref[:, cols] * scale * g_ref[:, cols]


def matmul_norm_resid(parts, w, g, x, tm=512):
    m = parts[0].shape[0]
    d = w.shape[1]
    tm = _pick_tile(m, tm)
    n_a = len(parts)
    kp = parts[0].shape[1]
    assert all(p.shape[1] == kp for p in parts) and n_a * kp == w.shape[0]
    assert _nbytes(w) <= RESIDENT_WEIGHT_BYTES
    in_specs = [pl.BlockSpec((tm, kp), lambda i: (i, 0)) for _ in parts]
    in_specs += [pl.BlockSpec((kp, d), lambda i, r=r: (r, 0), pipeline_mode=pl.Buffered(1)) for r in range(n_a)]
    in_specs += [_resident((1, d)), pl.BlockSpec((tm, d), lambda i: (i, 0))]
    return pl.pallas_call(
        functools.partial(_matmul_norm_resid_kernel, n_a=n_a, n_chunk=_pick_tile(d, N_CHUNK)),
        grid=(m // tm,),
        in_specs=in_specs,
        out_specs=pl.BlockSpec((tm, d), lambda i: (i, 0)),
        out_shape=jax.ShapeDtypeStruct((m, d), F32),
        compiler_params=_params("parallel"),
        name="matmul_norm_resid",
    )(*parts, *([w] * n_a), g.reshape(1, d).astype(F32), x)


def _rope_angles(pos, dim):
    inv = ROPE_THETA ** (-jnp.arange(0, dim, 2, dtype=F32) / dim)
    ang = pos.astype(F32)[:, None] * inv[None, :]
    return jnp.cos(ang), jnp.sin(ang)


def _retention_tables(t):
    return _rope_angles(jnp.arange(t), RET_DK)


def _axial_tables(t):
    rows = t // GRID_W
    row = jnp.repeat(jnp.arange(rows), GRID_W)
    col = jnp.tile(jnp.arange(GRID_W), rows)
    half = ATT_HD // 2
    cos_r, sin_r = _rope_angles(row, half)
    cos_c, sin_c = _rope_angles(col, half)
    cos = jnp.concatenate([cos_r, cos_r, cos_c, cos_c], axis=-1)
    sin = jnp.concatenate([-sin_r, sin_r, -sin_c, sin_c], axis=-1)
    return cos, sin


def _retention_kernel(lg_ref, q_ref, k_ref, v_ref, g_ref, cos_ref, sin_ref, o_ref,
                      qs_ref, ks_ref, acc_ref, st_ref, *, chunk):
    t = q_ref.shape[1]
    dk = q_ref.shape[2]
    hk = dk // 2
    nc = t // chunk
    h = pl.program_id(1)
    lg_f = lg_ref[h, 0]
    lg_b = lg_ref[h, 1]

    def rope(x, cos, sin):
        x1 = x[:, :hk]
        x2 = x[:, hk:]
        return jnp.concatenate([x1 * cos - x2 * sin, x2 * cos + x1 * sin], axis=-1)

    def prep(c, carry):
        rows = pl.ds(pl.multiple_of(c * chunk, chunk), chunk)
        qs_ref[rows, :] = rope(q_ref[0, rows, :].astype(F32), cos_ref[rows, :], sin_ref[rows, :])
        ks_ref[rows, :] = rope(k_ref[0, rows, :].astype(F32), cos_ref[rows, :], sin_ref[rows, :]) * (dk ** -0.5)
        return carry

    lax.fori_loop(0, nc, prep, 0)

    idx = lax.broadcasted_iota(jnp.int32, (chunk, 1), 0).astype(F32)
    ri = lax.broadcasted_iota(jnp.int32, (chunk, chunk), 0)
    ci = lax.broadcasted_iota(jnp.int32, (chunk, chunk), 1)
    diff = (ri - ci).astype(F32)
    d_f = jnp.where(ri >= ci, jnp.exp(lg_f * jnp.where(ri >= ci, diff, 0.0)), 0.0)
    d_b = jnp.where(ri < ci, jnp.exp(lg_b * jnp.where(ri < ci, -diff, 0.0)), 0.0)
    xi_f = jnp.exp(lg_f * (idx + 1.0))
    zeta_f = jnp.exp(lg_f * (chunk - 1.0 - idx))
    xi_b = jnp.exp(lg_b * (chunk - idx))
    zeta_b = jnp.exp(lg_b * idx)
    one = jnp.ones((1, 1), F32)
    gc_f = jnp.exp(one * (lg_f * chunk))
    gc_b = jnp.exp(one * (lg_b * chunk))

    def scores(rows):
        qc = qs_ref[rows, :]
        kc = ks_ref[rows, :]
        kb = kc.astype(BF16)
        return qc, kb, _dot_nt(qc.astype(BF16), kb)

    def step(d, rows, dmat, xi, zeta, gc, last):
        qc, kb, s = scores(rows)
        vc = v_ref[0, rows, :]
        inner = _dot((s * dmat).astype(BF16), vc)
        cross = _dot((qc * xi).astype(BF16), st_ref[d].astype(BF16))
        st_ref[d] = gc * st_ref[d] + _dot_tn(kb, (vc.astype(F32) * zeta).astype(BF16))
        if not last:
            acc_ref[rows, :] = inner + cross
        else:
            ret = acc_ref[rows, :] + inner + cross
            mu = jnp.mean(ret, axis=-1, keepdims=True)
            cen = ret - mu
            var = jnp.mean(cen * cen, axis=-1, keepdims=True)
            gate = _silu(g_ref[0, rows, :].astype(F32))
            o_ref[0, rows, :] = (gate * (cen * lax.rsqrt(var + EPS))).astype(o_ref.dtype)

    st_ref[...] = jnp.zeros_like(st_ref)
    half = nc // 2

    def scan(last):
        def body(i, carry):
            step(0, pl.ds(pl.multiple_of(i * chunk, chunk), chunk), d_f, xi_f, zeta_f, gc_f, last)
            step(1, pl.ds(pl.multiple_of((nc - 1 - i) * chunk, chunk), chunk), d_b, xi_b, zeta_b, gc_b, last)
            return carry
        return body

    lax.fori_loop(0, half, scan(False), 0, unroll=2)
    lax.fori_loop(half, nc, scan(True), 0, unroll=2)


def retention(proj, lg, cos, sin):
    b, t, _ = proj.shape
    hh = RET_HEADS
    chunk = _pick_tile(t // 2, RET_CHUNK)
    blk = lambda off: pl.BlockSpec((1, t, RET_DK), lambda bi, hi: (bi, 0, off + hi))
    return pl.pallas_call(
        functools.partial(_retention_kernel, chunk=chunk),
        grid=(b, hh),
        in_specs=[
            pl.BlockSpec(memory_space=pltpu.SMEM),
            blk(0), blk(hh), blk(2 * hh), blk(3 * hh),
            pl.BlockSpec((t, RET_DK // 2), lambda bi, hi: (0, 0)),
            pl.BlockSpec((t, RET_DK // 2), lambda bi, hi: (0, 0)),
        ],
        out_specs=pl.BlockSpec((1, t, RET_DV), lambda bi, hi: (bi, 0, hi)),
        out_shape=jax.ShapeDtypeStruct((b, t, hh * RET_DV), BF16),
        scratch_shapes=[
            pltpu.VMEM((t, RET_DK), F32),
            pltpu.VMEM((t, RET_DK), F32),
            pltpu.VMEM((t, RET_DV), F32),
            pltpu.VMEM((2, RET_DK, RET_DV), F32),
        ],
        compiler_params=_params("parallel", "parallel"),
        name="retention",
    )(lg, proj, proj, proj, proj, cos, sin)


def _axial_rope(x, cos, sin_signed):
    quarter = x.shape[-1] // 4
    lane = lax.broadcasted_iota(jnp.int32, x.shape, 1)
    first = (lane % (2 * quarter)) < quarter
    partner = jnp.where(first,
                        pltpu.roll(x, x.shape[-1] - quarter, axis=1),
                        pltpu.roll(x, quarter, axis=1))
    return x * cos + partner * sin_signed


def _gqa_kernel(q_ref, k_ref, v_ref, qg_ref, kg_ref, cosq_ref, sinq_ref, cosk_ref, sink_ref,
                o_ref, kt_ref):
    bq = q_ref.shape[1]
    hd = k_ref.shape[2]
    grp = q_ref.shape[2] // hd

    @pl.when(pl.program_id(2) == 0)
    def _():
        kn = _rms(k_ref[0].astype(F32), kg_ref[...])
        kt_ref[...] = _axial_rope(kn, cosk_ref[...], sink_ref[...]).astype(BF16)

    cos = cosq_ref[...]
    sin = sinq_ref[...]
    qs = []
    for h in range(grp):
        qn = _rms(q_ref[0, :, h * hd:(h + 1) * hd].astype(F32), qg_ref[...])
        qs.append((_axial_rope(qn, cos, sin) * (hd ** -0.5 * LOG2E)).astype(BF16))
    q = jnp.concatenate(qs, axis=0)
    t = kt_ref.shape[0]
    kb = t // _pick_tile(t // LANES, ATT_KV_BLOCKS)
    blocks = [slice(j, j + kb) for j in range(0, t, kb)]
    m = l = acc = None
    for blk in blocks:
        s = _dot_nt(q, kt_ref[blk, :])
        m_blk = jnp.max(s, axis=-1, keepdims=True)
        if m is None:
            m = m_blk
            p = jnp.exp2(s - m)
            l = jnp.sum(p, axis=-1, keepdims=True)
            acc = _dot(p.astype(BF16), v_ref[0, blk, :])
        else:
            m_new = jnp.maximum(m, m_blk)
            alpha = jnp.exp2(m - m_new)
            p = jnp.exp2(s - m_new)
            l = l * alpha + jnp.sum(p, axis=-1, keepdims=True)
            acc = acc * alpha + _dot(p.astype(BF16), v_ref[0, blk, :])
            m = m_new
    o = acc / l
    for h in range(grp):
        o_ref[0, :, h * hd:(h + 1) * hd] = o[h * bq:(h + 1) * bq].astype(o_ref.dtype)


def gqa_attention(proj, q_off, k_off, v_off, q_gain, k_gain, cos, sin):
    b, t, _ = proj.shape
    hd = ATT_HD
    gw = ATT_GROUP * hd
    bq = _pick_tile(t, ATT_BLOCK_Q)
    return pl.pallas_call(
        _gqa_kernel,
        grid=(b, ATT_KV_HEADS, t // bq),
        in_specs=[
            pl.BlockSpec((1, bq, gw), lambda bi, ki, qi: (bi, qi, q_off + ki)),
            pl.BlockSpec((1, t, hd), lambda bi, ki, qi: (bi, 0, k_off + ki)),
            pl.BlockSpec((1, t, hd), lambda bi, ki, qi: (bi, 0, v_off + ki)),
            pl.BlockSpec((1, hd), lambda bi, ki, qi: (0, 0)),
            pl.BlockSpec((1, hd), lambda bi, ki, qi: (0, 0)),
            pl.BlockSpec((bq, hd), lambda bi, ki, qi: (qi, 0)),
            pl.BlockSpec((bq, hd), lambda bi, ki, qi: (qi, 0)),
            pl.BlockSpec((t, hd), lambda bi, ki, qi: (0, 0)),
            pl.BlockSpec((t, hd), lambda bi, ki, qi: (0, 0)),
        ],
        out_specs=pl.BlockSpec((1, bq, gw), lambda bi, ki, qi: (bi, qi, ki)),
        out_shape=jax.ShapeDtypeStruct((b, t, ATT_HEADS * hd), BF16),
        scratch_shapes=[pltpu.VMEM((t, hd), BF16)],
        compiler_params=_params("parallel", "parallel", "arbitrary"),
        name="gqa_attention",
    )(proj, proj, proj, q_gain.reshape(1, hd).astype(F32), k_gain.reshape(1, hd).astype(F32),
      cos, sin, cos, sin)


def _cross_attn_kernel(q_ref, kv_ref, o_ref, *, heads):
    d = q_ref.shape[2]
    hd = d // heads
    for h in range(heads):
        qh = q_ref[0, :, h * hd:(h + 1) * hd]
        kh = kv_ref[0, :, h * hd:(h + 1) * hd]
        vh = kv_ref[0, :, d + h * hd:d + (h + 1) * hd]
        s = _dot_nt(qh, kh) * (hd ** -0.5)
        m = jnp.max(s, axis=-1, keepdims=True)
        p = jnp.exp(s - m)
        l = jnp.sum(p, axis=-1, keepdims=True)
        o_ref[0, :, h * hd:(h + 1) * hd] = (_dot(p.astype(BF16), vh) / l).astype(o_ref.dtype)


def cross_attention(q, kv):
    b, t, d = q.shape
    nm = kv.shape[1]
    tq = _pick_tile(t, XA_BLOCK_Q)
    return pl.pallas_call(
        functools.partial(_cross_attn_kernel, heads=XA_HEADS),
        grid=(b, t // tq),
        in_specs=[
            pl.BlockSpec((1, tq, d), lambda bi, qi: (bi, qi, 0)),
            pl.BlockSpec((1, nm, 2 * d), lambda bi, qi: (bi, 0, 0)),
        ],
        out_specs=pl.BlockSpec((1, tq, d), lambda bi, qi: (bi, qi, 0)),
        out_shape=jax.ShapeDtypeStruct((b, t, d), BF16),
        compiler_params=_params("parallel", "parallel"),
        name="cross_attention",
    )(q, kv)


def _in_proj_conv_kernel(x_ref, g_ref, w_ref, w2_ref, cw_ref, cb_ref, o_ref, o2_ref, h_ref, pad_ref,
                         *, z_tiles, n_sub, rows):
    j = pl.program_id(1)
    t = x_ref.shape[0]
    tn = o_ref.shape[1]
    sub = tn // n_sub
    halo = SUBLANES
    taps = cw_ref.shape[0]

    @pl.when(j == 0)
    def _():
        def norm_rows(i, carry):
            r = pl.ds(pl.multiple_of(i * rows, rows), rows)
            h_ref[r, :] = _rms(x_ref[r, :], g_ref[...]).astype(BF16)
            return carry

        lax.fori_loop(0, t // rows, norm_rows, 0)
        o2_ref[...] = _dot(h_ref[...], w2_ref[...])
        zeros = jnp.zeros((halo, sub), F32)
        for s in range(pad_ref.shape[0]):
            pad_ref[s, 0:halo, :] = zeros
            pad_ref[s, halo + t:2 * halo + t, :] = zeros

    @pl.when(j < z_tiles)
    def _():
        for c in range(n_sub):
            cols = slice(c * sub, (c + 1) * sub)
            o_ref[:, cols] = _dot(h_ref[...], w_ref[:, cols]).astype(o_ref.dtype)

    @pl.when(j >= z_tiles)
    def _():
        for c in range(n_sub):
            cols = slice(c * sub, (c + 1) * sub)
            slot = c % pad_ref.shape[0]
            pad_ref[slot, halo:halo + t, :] = _dot(h_ref[...], w_ref[:, cols])
            w = cw_ref[:, cols]
            bias = cb_ref[:, cols]
            for r0 in range(0, t, rows):
                win = pad_ref[slot, r0:r0 + rows + 2 * halo, :]
                acc = jnp.zeros((rows, sub), F32)
                for k in range(taps):
                    shift = (taps // 2 - k) % (rows + 2 * halo)
                    shifted = win if shift == 0 else pltpu.roll(win, shift, axis=0)
                    acc = acc + shifted[halo:halo + rows, :] * w[k:k + 1, :]
                o_ref[r0:r0 + rows, cols] = _silu(acc + bias).astype(o_ref.dtype)


def in_proj_conv(x, g, w, conv_w, conv_b, t, n_z, n_extra, tn=1024, n_sub=4):
    m, d = x.shape
    n = w.shape[1] - n_extra
    taps, c = conv_w.shape
    assert n == n_z + c and n_z % tn == 0 and c % tn == 0 and n % n_extra == 0 and m % t == 0
    z_tiles = n_z // tn
    rows = _pick_tile(t, CONV_ROWS)
    conv_col = lambda i, j: (0, jnp.maximum(j - z_tiles, 0))
    return pl.pallas_call(
        functools.partial(_in_proj_conv_kernel, z_tiles=z_tiles, n_sub=n_sub, rows=rows),
        grid=(m // t, n // tn),
        in_specs=[
            pl.BlockSpec((t, d), lambda i, j: (i, 0), pipeline_mode=pl.Buffered(1)),
            _resident((1, d)),
            pl.BlockSpec((d, tn), lambda i, j: (0, j)),
            pl.BlockSpec((d, n_extra), lambda i, j: (0, n // n_extra), pipeline_mode=pl.Buffered(1)),
            pl.BlockSpec((taps, tn), conv_col),
            pl.BlockSpec((1, tn), conv_col),
        ],
        out_specs=[pl.BlockSpec((t, tn), lambda i, j: (i, j)), pl.BlockSpec((t, n_extra), lambda i, j: (i, 0))],
        out_shape=[jax.ShapeDtypeStruct((m, n), BF16), jax.ShapeDtypeStruct((m, n_extra), F32)],
        scratch_shapes=[pltpu.VMEM((t, d), BF16), pltpu.VMEM((2, t + 2 * SUBLANES, tn // n_sub), F32)],
        compiler_params=_params("parallel", "arbitrary"),
        name="in_proj_conv",
    )(x, g.reshape(1, d).astype(F32), w, w, conv_w.astype(F32), conv_b.reshape(1, c).astype(F32))


def _dt_prep_kernel(raw_ref, bias_ref, alog_ref, dt_ref, acum_ref, *, chunk):
    t = raw_ref.shape[1]
    nh2 = raw_ref.shape[2]
    ri = lax.broadcasted_iota(jnp.int32, (chunk, chunk), 0)
    ci = lax.broadcasted_iota(jnp.int32, (chunk, chunk), 1)
    tri_f = (ci <= ri).astype(F32)
    tri_b = (ci >= ri).astype(F32)
    lane = lax.broadcasted_iota(jnp.int32, (chunk, nh2), 1)
    a = -jnp.exp(alog_ref[...])
    bias = bias_ref[...]

    def body(c, carry):
        rows = pl.ds(pl.multiple_of(c * chunk, chunk), chunk)
        dt = jax.nn.softplus(raw_ref[0, rows, :] + bias)
        dt_ref[0, rows, :] = dt
        da = dt * a
        pre = jnp.dot(tri_f, da, preferred_element_type=F32, precision=lax.Precision.HIGHEST)
        suf = jnp.dot(tri_b, da, preferred_element_type=F32, precision=lax.Precision.HIGHEST)
        acum_ref[0, rows, :] = jnp.where(lane < nh2 // 2, pre, suf)
        return carry

    lax.fori_loop(0, t // chunk, body, 0)


def dt_prep(dt_raw, dt_bias, a_log):
    b, t, nh2 = dt_raw.shape
    chunk = _pick_tile(t // 2, SSD_CHUNK)
    spec = pl.BlockSpec((1, t, nh2), lambda bi: (bi, 0, 0))
    vec = pl.BlockSpec((1, nh2), lambda bi: (0, 0))
    return pl.pallas_call(
        functools.partial(_dt_prep_kernel, chunk=chunk),
        grid=(b,),
        in_specs=[spec, vec, vec],
        out_specs=[spec, spec],
        out_shape=[jax.ShapeDtypeStruct((b, t, nh2), F32)] * 2,
        compiler_params=_params("parallel"),
        name="ssd_dt_prep",
    )(dt_raw, dt_bias.reshape(1, nh2).astype(F32), a_log.reshape(1, nh2).astype(F32))


def _ssd_kernel(x_ref, b_ref, c_ref, z_ref, acum_ref, dt_ref, arow_ref, drow_ref, dskip_ref, ng_ref,
                o_ref, yacc_ref, st_ref, *, chunk):
    t = x_ref.shape[1]
    width = x_ref.shape[2]
    npair = width // LANES
    hp = LANES // 2
    heads = 2 * npair
    nc = t // chunk
    half_lanes = acum_ref.shape[2] // 2
    grp = pl.program_id(1)
    ri = lax.broadcasted_iota(jnp.int32, (chunk, chunk), 0)
    ci = lax.broadcasted_iota(jnp.int32, (chunk, chunk), 1)
    lane = lax.broadcasted_iota(jnp.int32, (chunk, LANES), 1)
    lo = lane < hp
    hi_i = (lane >= hp).astype(jnp.int32)

    def lane_take(a, idx):
        return jnp.take_along_axis(a, idx, axis=1, mode="promise_in_bounds")

    def dir_chunk(d, c, last):
        mask = (ri >= ci) if d == 0 else (ri <= ci)
        rows = pl.ds(pl.multiple_of(c * chunk, chunk), chunk)
        x = x_ref[0, rows, :].astype(F32)
        bc = b_ref[0, rows, :]
        cc = c_ref[0, rows, :]
        a_all = acum_ref[0, rows, :] * LOG2E
        dt_all = dt_ref[0, rows, :]
        a_end = a_all[chunk - 1:chunk, :] if d == 0 else a_all[0:1, :]
        w_all = jnp.exp2(a_end - a_all) * dt_all
        ee_all = jnp.broadcast_to(jnp.exp2(a_end), (SUBLANES, a_all.shape[1]))
        arow = arow_ref[0, d, 0, c] * LOG2E - jnp.log2(drow_ref[0, d, 0, c])
        base = d * half_lanes + grp * heads

        cb = _dot_nt(cc, bc)
        coff = _dot(cc, st_ref[d].astype(BF16))

        ys, xws, ends = [], [], []
        for p in range(npair):
            pair_idx = base + 2 * p + hi_i
            xt = x[:, p * LANES:(p + 1) * LANES]
            a_cols = [lane_take(a_all, jnp.zeros_like(lane) + (base + 2 * p + half)) for half in range(2)]
            y = coff[:, p * LANES:(p + 1) * LANES] * jnp.exp2(jnp.where(lo, a_cols[0], a_cols[1]))
            for half in range(2):
                k = 2 * p + half
                seg = a_cols[half] - arow[k:k + 1, :]
                mk = (cb * jnp.exp2(jnp.where(mask, seg, NEG_BIG))).astype(BF16)
                keep = lo if half == 0 else jnp.logical_not(lo)
                y = y + _dot(mk, jnp.where(keep, xt, 0.0).astype(BF16))
            ys.append(y)
            xws.append((xt * lane_take(w_all, pair_idx)).astype(BF16))
            ends.append(lane_take(ee_all, pair_idx[:SUBLANES])[0:1])
        y = jnp.concatenate(ys, axis=-1)
        xw = jnp.concatenate(xws, axis=-1)
        e_row = jnp.concatenate(ends, axis=-1)
        st_ref[d] = st_ref[d] * e_row + _dot_tn(bc, xw)
        if not last:
            yacc_ref[rows, :] = y
        else:
            tot = yacc_ref[rows, :] + y + dskip_ref[...] * x
            tot = tot * _silu(z_ref[0, rows, :].astype(F32))
            o_ref[0, rows, :] = _rms(tot, ng_ref[...]).astype(o_ref.dtype)

    st_ref[...] = jnp.zeros_like(st_ref)
    half = nc // 2

    def scan(last):
        def body(i, carry):
            dir_chunk(0, i, last)
            dir_chunk(1, nc - 1 - i, last)
            return carry
        return body

    lax.fori_loop(0, half, scan(False), 0, unroll=2)
    lax.fori_loop(half, nc, scan(True), 0, unroll=2)


def ssd_scan(xbc, x_col, zx, acum, dt, arow, drow, d_skip_wide, norm_g):
    b, t, _ = xbc.shape
    g = SSD_GROUPS
    n = SSD_STATE
    inner = d_skip_wide.shape[0]
    width = inner // g
    kh = width // SSD_HEADDIM
    chunk = _pick_tile(t // 2, SSD_CHUNK)
    x_off = x_col // width
    b_off = (x_col + inner) // n
    c_off = b_off + g
    nh2 = acum.shape[2]
    all_heads = pl.BlockSpec((1, t, nh2), lambda bi, gi: (bi, 0, 0))
    row_spec = pl.BlockSpec((1, 2, 1, t // chunk, kh, chunk), lambda bi, gi: (bi, 0, gi, 0, 0, 0))
    return pl.pallas_call(
        functools.partial(_ssd_kernel, chunk=chunk),
        grid=(b, g),
        in_specs=[
            pl.BlockSpec((1, t, width), lambda bi, gi: (bi, 0, x_off + gi)),
            pl.BlockSpec((1, t, n), lambda bi, gi: (bi, 0, b_off + gi)),
            pl.BlockSpec((1, t, n), lambda bi, gi: (bi, 0, c_off + gi)),
            pl.BlockSpec((1, t, width), lambda bi, gi: (bi, 0, gi)),
            all_heads, all_heads, row_spec, row_spec,
            pl.BlockSpec((1, width), lambda bi, gi: (0, gi)),
            pl.BlockSpec((1, width), lambda bi, gi: (0, gi)),
        ],
        out_specs=pl.BlockSpec((1, t, width), lambda bi, gi: (bi, 0, gi)),
        out_shape=jax.ShapeDtypeStruct((b, t, inner), BF16),
        scratch_shapes=[pltpu.VMEM((t, width), F32), pltpu.VMEM((2, n, width), F32)],
        compiler_params=_params("parallel", "arbitrary"),
        name="ssd_scan",
    )(xbc, xbc, xbc, zx, acum, dt, arow, drow,
      d_skip_wide.reshape(1, inner).astype(F32), norm_g.reshape(1, inner).astype(F32))


def _even_mixer(xf, b, t, gn, w_in, q_gain, k_gain, w_out):
    d = xf.shape[1]
    proj = norm_matmul(xf, gn[0], w_in).reshape(b, t, -1)
    heads = jnp.arange(RET_HEADS, dtype=F32)
    lg = jnp.stack([jnp.log1p(-jnp.exp2(-(RET_DECAY_EXP_FWD + heads))),
                    jnp.log1p(-jnp.exp2(-(RET_DECAY_EXP_BWD + heads)))], axis=1)
    cos1, sin1 = _retention_tables(t)
    ret = retention(proj, lg, cos1, sin1)
    cos2, sin2 = _axial_tables(t)
    att_base = 2 * RET_HEADS * RET_DK + 2 * RET_HEADS * RET_DV
    gw = ATT_GROUP * ATT_HD
    k_base = att_base + ATT_HEADS * ATT_HD
    v_base = k_base + ATT_KV_HEADS * ATT_HD
    att = gqa_attention(proj, att_base // gw, k_base // ATT_HD, v_base // ATT_HD,
                        q_gain, k_gain, cos2, sin2)
    parts = [ret.reshape(b * t, -1), att.reshape(b * t, -1)]
    return matmul_norm_resid(parts, w_out, gn[1], xf)


def _odd_mixer(xf, b, t, gn, w_in, conv_w, conv_b, a_log, dt_bias, d_skip, norm_g, w_out):
    inner = norm_g.shape[0]
    nh = d_skip.shape[0]
    g = SSD_GROUPS
    kh = nh // g
    zx, dt_raw = in_proj_conv(xf, gn[0], w_in, conv_w, conv_b, t, inner, 2 * nh)
    zx = zx.reshape(b, t, -1)
    dt_raw = dt_raw.reshape(b, t, 2 * nh)
    dt, acum = dt_prep(dt_raw, dt_bias.reshape(-1), a_log.reshape(-1))
    chunk = _pick_tile(t // 2, SSD_CHUNK)
    rows_of = lambda arr: arr.reshape(b, t // chunk, chunk, 2, g, kh).transpose(0, 3, 4, 1, 5, 2)
    y = ssd_scan(zx, inner, zx, acum, dt, rows_of(acum), rows_of(dt),
                 jnp.repeat(d_skip.astype(F32), inner // nh), norm_g)
    return matmul_norm_resid([y.reshape(b * t, inner)], w_out, gn[1], xf)


def _trunk(x, mem, norm_g, wts):
    b, t, d = x.shape
    nm = mem.shape[1]
    xf = x.reshape(b * t, d)
    memf = mem.reshape(b * nm, d)
    depth = norm_g.shape[0]
    for i in range(depth):
        gn = norm_g[i]
        j = i // 2
        if i % 2 == 0:
            xf = _even_mixer(xf, b, t, gn, wts["ev_w_in"][j], wts["ev_q_gain"][j], wts["ev_k_gain"][j],
                             wts["ev_w_out"][j])
        else:
            xf = _odd_mixer(xf, b, t, gn, wts["od_w_in"][j], wts["od_conv_w"][j],
                            wts["od_conv_b"][j], wts["od_a_log"][j], wts["od_dt_bias"][j], wts["od_d"][j],
                            wts["od_norm_g"][j], wts["od_w_out"][j])
        q = norm_matmul(xf, gn[2], wts["xa_wq"][i]).reshape(b, t, d)
        kv = norm_matmul(memf, gn[4], wts["xa_wkv"][i]).reshape(b, nm, 2 * d)
        xa = cross_attention(q, kv).reshape(b * t, d)
        xf = matmul_norm_resid([xa], wts["xa_wo"][i], gn[3], xf)
        act = norm_swiglu(xf, gn[5], wts["ffn_w_gu"][i])
        xf = matmul_norm_resid([act], wts["ffn_w_down"][i], gn[6], xf)
    return xf.reshape(b, t, d)


def kernel(x_prompt, x_sample, mem_prompt, mem_sample, norm_g, ev_w_in, ev_q_gain, ev_k_gain, ev_w_out,
           od_w_in, od_conv_w, od_conv_b, od_a_log, od_dt_bias, od_d, od_norm_g, od_w_out,
           xa_wq, xa_wkv, xa_wo, ffn_w_gu, ffn_w_down):
    wts = dict(
        ev_w_in=ev_w_in.astype(BF16), ev_q_gain=ev_q_gain, ev_k_gain=ev_k_gain,
        ev_w_out=ev_w_out.astype(BF16),
        od_w_in=od_w_in.astype(BF16),
        od_conv_w=od_conv_w, od_conv_b=od_conv_b, od_a_log=od_a_log, od_dt_bias=od_dt_bias,
        od_d=od_d, od_norm_g=od_norm_g, od_w_out=od_w_out.astype(BF16),
        xa_wq=xa_wq.astype(BF16), xa_wkv=xa_wkv.astype(BF16), xa_wo=xa_wo.astype(BF16),
        ffn_w_gu=ffn_w_gu.astype(BF16), ffn_w_down=ffn_w_down.astype(BF16),
    )
    y_prompt = _trunk(x_prompt, mem_prompt, norm_g, wts)
    y_sample = _trunk(x_sample, mem_sample, norm_g, wts)
    return (y_prompt, y_sample)
```

```python
import functools

import jax
import jax.numpy as jnp
from jax import lax
from jax.experimental import pallas as pl
from jax.experimental.pallas import tpu as pltpu

F32 = jnp.float32
BF16 = jnp.bfloat16

EPS = 1e-6
ROPE_THETA = 10000.0
GRID_W = 64

RET_HEADS = 4
RET_DK = 256
RET_DV = 256
RET_DECAY_EXP_FWD = 5.0
RET_DECAY_EXP_BWD = 5.5
ATT_HEADS = 8
ATT_KV_HEADS = 2
ATT_HD = 128
ATT_GROUP = ATT_HEADS // ATT_KV_HEADS

SSD_HEADDIM = 64
SSD_STATE = 128
SSD_GROUPS = 8
SSD_CONV = 5
SSD_CHUNK = 128

XA_HEADS = 4

VMEM_LIMIT_BYTES = 56 * 1024 * 1024
LANES = 128
SUBLANES = 8

RET_CHUNK = 256
ATT_BLOCK_Q = 128
XA_BLOCK_Q = 1024
CONV_ROWS = 256
NEG_BIG = -1e30
RESIDENT_WEIGHT_BYTES = 24 * 1024 * 1024
N_CHUNK = 512
ATT_KV_BLOCKS = 2
LOG2E = 1.4426950408889634


def _params(*sem):
    return pltpu.CompilerParams(dimension_semantics=sem, vmem_limit_bytes=VMEM_LIMIT_BYTES)


def _dot(a, b):
    return jnp.dot(a, b, preferred_element_type=F32)


def _dot_nt(a, b):
    return lax.dot_general(a, b, (((1,), (1,)), ((), ())), preferred_element_type=F32)


def _dot_tn(a, b):
    return lax.dot_general(a, b, (((0,), (0,)), ((), ())), preferred_element_type=F32)


def _silu(x):
    h = 0.5 * x
    return h + h * jnp.tanh(h)


def _rms(x, g):
    return x * lax.rsqrt(jnp.mean(x * x, axis=-1, keepdims=True) + EPS) * g


def _pick_tile(n, pref):
    t = min(pref, n)
    while n % t:
        t //= 2
    return t


def _resident(shape):
    zeros = (0,) * len(shape)
    return pl.BlockSpec(shape, lambda *_: zeros, pipeline_mode=pl.Buffered(1))


def _nbytes(a):
    return a.size * a.dtype.itemsize


def _norm_matmul_resident_kernel(x_ref, g_ref, w_ref, o_ref, h_ref, *, n_chunk):
    h_ref[...] = _rms(x_ref[...], g_ref[...]).astype(BF16)
    for c0 in range(0, w_ref.shape[1], n_chunk):
        cols = slice(c0, c0 + n_chunk)
        o_ref[:, cols] = _dot(h_ref[...], w_ref[:, cols]).astype(o_ref.dtype)


def norm_matmul(x, g, w):
    m, d = x.shape
    n = w.shape[1]
    assert _nbytes(w) <= RESIDENT_WEIGHT_BYTES
    tm = _pick_tile(m, 1024 if _nbytes(w) <= RESIDENT_WEIGHT_BYTES // 3 else 512)
    return pl.pallas_call(
        functools.partial(_norm_matmul_resident_kernel, n_chunk=_pick_tile(n, N_CHUNK)),
        grid=(m // tm,),
        in_specs=[pl.BlockSpec((tm, d), lambda i: (i, 0)), _resident((1, d)), _resident((d, n))],
        out_specs=pl.BlockSpec((tm, n), lambda i: (i, 0)),
        out_shape=jax.ShapeDtypeStruct((m, n), BF16),
        scratch_shapes=[pltpu.VMEM((tm, d), BF16)],
        compiler_params=_params("parallel"),
        name="norm_matmul_resident",
    )(x, g.reshape(1, d).astype(F32), w)


def _norm_swiglu_kernel(x_ref, g_ref, wg_ref, wu_ref, o_ref, h_ref, *, n_chunk):
    h_ref[...] = _rms(x_ref[...], g_ref[...]).astype(BF16)
    for c0 in range(0, o_ref.shape[1], n_chunk):
        cols = slice(c0, c0 + n_chunk)
        gate = _dot(h_ref[...], wg_ref[:, cols])
        up = _dot(h_ref[...], wu_ref[:, cols])
        o_ref[:, cols] = (_silu(gate) * up).astype(o_ref.dtype)


def norm_swiglu(x, g, w_gu, tm=512):
    m, d = x.shape
    f = w_gu.shape[1] // 2
    parts = 1
    while 2 * d * (f // parts) * w_gu.dtype.itemsize > RESIDENT_WEIGHT_BYTES:
        parts *= 2
    fp = f // parts
    assert fp * parts == f and fp % LANES == 0
    tm = _pick_tile(m, tm)
    g2 = g.reshape(1, d).astype(F32)
    outs = []
    for part in range(parts):
        col = lambda off: pl.BlockSpec((d, fp), lambda i, off=off: (0, off), pipeline_mode=pl.Buffered(1))
        outs.append(pl.pallas_call(
            functools.partial(_norm_swiglu_kernel, n_chunk=_pick_tile(fp, N_CHUNK)),
            grid=(m // tm,),
            in_specs=[pl.BlockSpec((tm, d), lambda i: (i, 0)), _resident((1, d)), col(part), col(parts + part)],
            out_specs=pl.BlockSpec((tm, fp), lambda i: (i, 0)),
            out_shape=jax.ShapeDtypeStruct((m, fp), BF16),
            scratch_shapes=[pltpu.VMEM((tm, d), BF16)],
            compiler_params=_params("parallel"),
            name="norm_swiglu",
        )(x, g2, w_gu, w_gu))
    return outs


def _matmul_norm_resid_kernel(*refs, n_a, n_chunk):
    a_refs = refs[:n_a]
    w_refs = refs[n_a:2 * n_a]
    g_ref, x_ref, o_ref = refs[2 * n_a:]
    tm, d = o_ref.shape
    ss = jnp.zeros((tm, 1), F32)
    for c0 in range(0, d, n_chunk):
        cols = slice(c0, c0 + n_chunk)
        y = _dot(a_refs[0][...], w_refs[0][:, cols])
        for a_ref, w_ref in zip(a_refs[1:], w_refs[1:]):
            y = y + _dot(a_ref[...], w_ref[:, cols])
        o_ref[:, cols] = y
        ss = ss + jnp.sum(y * y, axis=-1, keepdims=True)
    scale = lax.rsqrt(ss / d + EPS)
    for c0 in range(0, d, n_chunk):
        cols = slice(c0, c0 + n_chunk)
        o_ref[:, cols] = x_ref[:, cols] + o_ref[:, cols] * scale * g_ref[:, cols]


def matmul_norm_resid(parts, w, g, x, tm=512):
    m = parts[0].shape[0]
    d = w.shape[1]
    tm = _pick_tile(m, tm)
    n_a = len(parts)
    kp = parts[0].shape[1]
    assert all(p.shape[1] == kp for p in parts) and n_a * kp == w.shape[0]
    assert _nbytes(w) <= RESIDENT_WEIGHT_BYTES
    in_specs = [pl.BlockSpec((tm, kp), lambda i: (i, 0)) for _ in parts]
    in_specs += [pl.BlockSpec((kp, d), lambda i, r=r: (r, 0), pipeline_mode=pl.Buffered(1)) for r in range(n_a)]
    in_specs += [_resident((1, d)), pl.BlockSpec((tm, d), lambda i: (i, 0))]
    return pl.pallas_call(
        functools.partial(_matmul_norm_resid_kernel, n_a=n_a, n_chunk=_pick_tile(d, N_CHUNK)),
        grid=(m // tm,),
        in_specs=in_specs,
        out_specs=pl.BlockSpec((tm, d), lambda i: (i, 0)),
        out_shape=jax.ShapeDtypeStruct((m, d), F32),
        compiler_params=_params("parallel"),
        name="matmul_norm_resid",
    )(*parts, *([w] * n_a), g.reshape(1, d).astype(F32), x)


def _rope_angles(pos, dim):
    inv = ROPE_THETA ** (-jnp.arange(0, dim, 2, dtype=F32) / dim)
    ang = pos.astype(F32)[:, None] * inv[None, :]
    return jnp.cos(ang), jnp.sin(ang)


def _retention_tables(t):
    return _rope_angles(jnp.arange(t), RET_DK)


def _axial_tables(t):
    rows = t // GRID_W
    row = jnp.repeat(jnp.arange(rows), GRID_W)
    col = jnp.tile(jnp.arange(GRID_W), rows)
    half = ATT_HD // 2
    cos_r, sin_r = _rope_angles(row, half)
    cos_c, sin_c = _rope_angles(col, half)
    cos = jnp.concatenate([cos_r, cos_r, cos_c, cos_c], axis=-1)
    sin = jnp.concatenate([-sin_r, sin_r, -sin_c, sin_c], axis=-1)
    return cos, sin


def _retention_kernel(lg_ref, q_ref, k_ref, v_ref, g_ref, cos_ref, sin_ref, o_ref,
                      qs_ref, ks_ref, acc_ref, st_ref, *, chunk):
    t = q_ref.shape[1]
    dk = q_ref.shape[2]
    hk = dk // 2
    nc = t // chunk
    h = pl.program_id(1)
    lg_f = lg_ref[h, 0]
    lg_b = lg_ref[h, 1]

    def rope(x, cos, sin):
        x1 = x[:, :hk]
        x2 = x[:, hk:]
        return jnp.concatenate([x1 * cos - x2 * sin, x2 * cos + x1 * sin], axis=-1)

    def prep(c, carry):
        rows = pl.ds(pl.multiple_of(c * chunk, chunk), chunk)
        qs_ref[rows, :] = rope(q_ref[0, rows, :].astype(F32), cos_ref[rows, :], sin_ref[rows, :])
        ks_ref[rows, :] = rope(k_ref[0, rows, :].astype(F32), cos_ref[rows, :], sin_ref[rows, :]) * (dk ** -0.5)
        return carry

    lax.fori_loop(0, nc, prep, 0)

    idx = lax.broadcasted_iota(jnp.int32, (chunk, 1), 0).astype(F32)
    ri = lax.broadcasted_iota(jnp.int32, (chunk, chunk), 0)
    ci = lax.broadcasted_iota(jnp.int32, (chunk, chunk), 1)
    diff = (ri - ci).astype(F32)
    d_f = jnp.where(ri >= ci, jnp.exp(lg_f * jnp.where(ri >= ci, diff, 0.0)), 0.0)
    d_b = jnp.where(ri < ci, jnp.exp(lg_b * jnp.where(ri < ci, -diff, 0.0)), 0.0)
    xi_f = jnp.exp(lg_f * (idx + 1.0))
    zeta_f = jnp.exp(lg_f * (chunk - 1.0 - idx))
    xi_b = jnp.exp(lg_b * (chunk - idx))
    zeta_b = jnp.exp(lg_b * idx)
    one = jnp.ones((1, 1), F32)
    gc_f = jnp.exp(one * (lg_f * chunk))
    gc_b = jnp.exp(one * (lg_b * chunk))

    def scores(rows):
        qc = qs_ref[rows, :]
        kc = ks_ref[rows, :]
        kb = kc.astype(BF16)
        return qc, kb, _dot_nt(qc.astype(BF16), kb)

    def step(d, rows, dmat, xi, zeta, gc, last):
        qc, kb, s = scores(rows)
        vc = v_ref[0, rows, :]
        inner = _dot((s * dmat).astype(BF16), vc)
        cross = _dot((qc * xi).astype(BF16), st_ref[d].astype(BF16))
        st_ref[d] = gc * st_ref[d] + _dot_tn(kb, (vc.astype(F32) * zeta).astype(BF16))
        if not last:
            acc_ref[rows, :] = inner + cross
        else:
            ret = acc_ref[rows, :] + inner + cross
            mu = jnp.mean(ret, axis=-1, keepdims=True)
            cen = ret - mu
            var = jnp.mean(cen * cen, axis=-1, keepdims=True)
            gate = _silu(g_ref[0, rows, :].astype(F32))
            o_ref[0, rows, :] = (gate * (cen * lax.rsqrt(var + EPS))).astype(o_ref.dtype)

    st_ref[...] = jnp.zeros_like(st_ref)
    half = nc // 2

    def scan(last):
        def body(i, carry):
            step(0, pl.ds(pl.multiple_of(i * chunk, chunk), chunk), d_f, xi_f, zeta_f, gc_f, last)
            step(1, pl.ds(pl.multiple_of((nc - 1 - i) * chunk, chunk), chunk), d_b, xi_b, zeta_b, gc_b, last)
            return carry
        return body

    lax.fori_loop(0, half, scan(False), 0, unroll=2)
    lax.fori_loop(half, nc, scan(True), 0, unroll=2)


def retention(proj, lg, cos, sin):
    b, t, _ = proj.shape
    hh = RET_HEADS
    chunk = _pick_tile(t // 2, RET_CHUNK)
    blk = lambda off: pl.BlockSpec((1, t, RET_DK), lambda bi, hi: (bi, 0, off + hi))
    return pl.pallas_call(
        functools.partial(_retention_kernel, chunk=chunk),
        grid=(b, hh),
        in_specs=[
            pl.BlockSpec(memory_space=pltpu.SMEM),
            blk(0), blk(hh), blk(2 * hh), blk(3 * hh),
            pl.BlockSpec((t, RET_DK // 2), lambda bi, hi: (0, 0)),
            pl.BlockSpec((t, RET_DK // 2), lambda bi, hi: (0, 0)),
        ],
        out_specs=pl.BlockSpec((1, t, RET_DV), lambda bi, hi: (bi, 0, hi)),
        out_shape=jax.ShapeDtypeStruct((b, t, hh * RET_DV), BF16),
        scratch_shapes=[
            pltpu.VMEM((t, RET_DK), F32),
            pltpu.VMEM((t, RET_DK), F32),
            pltpu.VMEM((t, RET_DV), F32),
            pltpu.VMEM((2, RET_DK, RET_DV), F32),
        ],
        compiler_params=_params("parallel", "parallel"),
        name="retention",
    )(lg, proj, proj, proj, proj, cos, sin)


def _axial_rope(x, cos, sin_signed):
    quarter = x.shape[-1] // 4
    lane = lax.broadcasted_iota(jnp.int32, x.shape, 1)
    first = (lane % (2 * quarter)) < quarter
    partner = jnp.where(first,
                        pltpu.roll(x, x.shape[-1] - quarter, axis=1),
                        pltpu.roll(x, quarter, axis=1))
    return x * cos + partner * sin_signed


def _gqa_kernel(q_ref, k_ref, v_ref, qg_ref, kg_ref, cosq_ref, sinq_ref, cosk_ref, sink_ref,
                o_ref, kt_ref):
    bq = q_ref.shape[1]
    hd = k_ref.shape[2]
    grp = q_ref.shape[2] // hd

    @pl.when(pl.program_id(2) == 0)
    def _():
        kn = _rms(k_ref[0].astype(F32), kg_ref[...])
        kt_ref[...] = _axial_rope(kn, cosk_ref[...], sink_ref[...]).astype(BF16)

    cos = cosq_ref[...]
    sin = sinq_ref[...]
    qs = []
    for h in range(grp):
        qn = _rms(q_ref[0, :, h * hd:(h + 1) * hd].astype(F32), qg_ref[...])
        qs.append((_axial_rope(qn, cos, sin) * (hd ** -0.5 * LOG2E)).astype(BF16))
    q = jnp.concatenate(qs, axis=0)
    t = kt_ref.shape[0]
    kb = t // _pick_tile(t // LANES, ATT_KV_BLOCKS)
    blocks = [slice(j, j + kb) for j in range(0, t, kb)]
    m = l = acc = None
    for blk in blocks:
        s = _dot_nt(q, kt_ref[blk, :])
        m_blk = jnp.max(s, axis=-1, keepdims=True)
        if m is None:
            m = m_blk
            p = jnp.exp2(s - m)
            l = jnp.sum(p, axis=-1, keepdims=True)
            acc = _dot(p.astype(BF16), v_ref[0, blk, :])
        else:
            m_new = jnp.maximum(m, m_blk)
            alpha = jnp.exp2(m - m_new)
            p = jnp.exp2(s - m_new)
            l = l * alpha + jnp.sum(p, axis=-1, keepdims=True)
            acc = acc * alpha + _dot(p.astype(BF16), v_ref[0, blk, :])
            m = m_new
    o = acc / l
    for h in range(grp):
        o_ref[0, :, h * hd:(h + 1) * hd] = o[h * bq:(h + 1) * bq].astype(o_ref.dtype)


def gqa_attention(proj, q_off, k_off, v_off, q_gain, k_gain, cos, sin):
    b, t, _ = proj.shape
    hd = ATT_HD
    gw = ATT_GROUP * hd
    bq = _pick_tile(t, ATT_BLOCK_Q)
    return pl.pallas_call(
        _gqa_kernel,
        grid=(b, ATT_KV_HEADS, t // bq),
        in_specs=[
            pl.BlockSpec((1, bq, gw), lambda bi, ki, qi: (bi, qi, q_off + ki)),
            pl.BlockSpec((1, t, hd), lambda bi, ki, qi: (bi, 0, k_off + ki)),
            pl.BlockSpec((1, t, hd), lambda bi, ki, qi: (bi, 0, v_off + ki)),
            pl.BlockSpec((1, hd), lambda bi, ki, qi: (0, 0)),
            pl.BlockSpec((1, hd), lambda bi, ki, qi: (0, 0)),
            pl.BlockSpec((bq, hd), lambda bi, ki, qi: (qi, 0)),
            pl.BlockSpec((bq, hd), lambda bi, ki, qi: (qi, 0)),
            pl.BlockSpec((t, hd), lambda bi, ki, qi: (0, 0)),
            pl.BlockSpec((t, hd), lambda bi, ki, qi: (0, 0)),
        ],
        out_specs=pl.BlockSpec((1, bq, gw), lambda bi, ki, qi: (bi, qi, ki)),
        out_shape=jax.ShapeDtypeStruct((b, t, ATT_HEADS * hd), BF16),
        scratch_shapes=[pltpu.VMEM((t, hd), BF16)],
        compiler_params=_params("parallel", "parallel", "arbitrary"),
        name="gqa_attention",
    )(proj, proj, proj, q_gain.reshape(1, hd).astype(F32), k_gain.reshape(1, hd).astype(F32),
      cos, sin, cos, sin)


def _cross_attn_kernel(q_ref, kv_ref, o_ref, *, heads):
    d = q_ref.shape[2]
    hd = d // heads
    for h in range(heads):
        qh = q_ref[0, :, h * hd:(h + 1) * hd]
        kh = kv_ref[0, :, h * hd:(h + 1) * hd]
        vh = kv_ref[0, :, d + h * hd:d + (h + 1) * hd]
        s = _dot_nt(qh, kh) * (hd ** -0.5)
        m = jnp.max(s, axis=-1, keepdims=True)
        p = jnp.exp(s - m)
        l = jnp.sum(p, axis=-1, keepdims=True)
        o_ref[0, :, h * hd:(h + 1) * hd] = (_dot(p.astype(BF16), vh) / l).astype(o_ref.dtype)


def cross_attention(q, kv):
    b, t, d = q.shape
    nm = kv.shape[1]
    tq = _pick_tile(t, XA_BLOCK_Q)
    return pl.pallas_call(
        functools.partial(_cross_attn_kernel, heads=XA_HEADS),
        grid=(b, t // tq),
        in_specs=[
            pl.BlockSpec((1, tq, d), lambda bi, qi: (bi, qi, 0)),
            pl.BlockSpec((1, nm, 2 * d), lambda bi, qi: (bi, 0, 0)),
        ],
        out_specs=pl.BlockSpec((1, tq, d), lambda bi, qi: (bi, qi, 0)),
        out_shape=jax.ShapeDtypeStruct((b, t, d), BF16),
        compiler_params=_params("parallel", "parallel"),
        name="cross_attention",
    )(q, kv)


def _in_proj_conv_kernel(x_ref, g_ref, w_ref, w2_ref, cw_ref, cb_ref, o_ref, o2_ref, h_ref, pad_ref,
                         *, z_tiles, n_sub, rows):
    j = pl.program_id(1)
    t = x_ref.shape[0]
    tn = o_ref.shape[1]
    sub = tn // n_sub
    halo = SUBLANES
    taps = cw_ref.shape[0]

    @pl.when(j == 0)
    def _():
        def norm_rows(i, carry):
            r = pl.ds(pl.multiple_of(i * rows, rows), rows)
            h_ref[r, :] = _rms(x_ref[r, :], g_ref[...]).astype(BF16)
            return carry

        lax.fori_loop(0, t // rows, norm_rows, 0)
        o2_ref[...] = _dot(h_ref[...], w2_ref[...])
        zeros = jnp.zeros((halo, sub), F32)
        for s in range(pad_ref.shape[0]):
            pad_ref[s, 0:halo, :] = zeros
            pad_ref[s, halo + t:2 * halo + t, :] = zeros

    @pl.when(j < z_tiles)
    def _():
        for c in range(n_sub):
            cols = slice(c * sub, (c + 1) * sub)
            o_ref[:, cols] = _dot(h_ref[...], w_ref[:, cols]).astype(o_ref.dtype)

    @pl.when(j >= z_tiles)
    def _():
        for c in range(n_sub):
            cols = slice(c * sub, (c + 1) * sub)
            slot = c % pad_ref.shape[0]
            pad_ref[slot, halo:halo + t, :] = _dot(h_ref[...], w_ref[:, cols])
            w = cw_ref[:, cols]
            bias = cb_ref[:, cols]
            for r0 in range(0, t, rows):
                win = pad_ref[slot, r0:r0 + rows + 2 * halo, :]
                acc = jnp.zeros((rows, sub), F32)
                for k in range(taps):
                    shift = (taps // 2 - k) % (rows + 2 * halo)
                    shifted = win if shift == 0 else pltpu.roll(win, shift, axis=0)
                    acc = acc + shifted[halo:halo + rows, :] * w[k:k + 1, :]
                o_ref[r0:r0 + rows, cols] = _silu(acc + bias).astype(o_ref.dtype)


def in_proj_conv(x, g, w, conv_w, conv_b, t, n_z, n_extra, tn=1024, n_sub=4):
    m, d = x.shape
    n = w.shape[1] - n_extra
    taps, c = conv_w.shape
    assert n == n_z + c and n_z % tn == 0 and c % tn == 0 and n % n_extra == 0 and m % t == 0
    z_tiles = n_z // tn
    rows = _pick_tile(t, CONV_ROWS)
    conv_col = lambda i, j: (0, jnp.maximum(j - z_tiles, 0))
    return pl.pallas_call(
        functools.partial(_in_proj_conv_kernel, z_tiles=z_tiles, n_sub=n_sub, rows=rows),
        grid=(m // t, n // tn),
        in_specs=[
            pl.BlockSpec((t, d), lambda i, j: (i, 0), pipeline_mode=pl.Buffered(1)),
            _resident((1, d)),
            pl.BlockSpec((d, tn), lambda i, j: (0, j)),
            pl.BlockSpec((d, n_extra), lambda i, j: (0, n // n_extra), pipeline_mode=pl.Buffered(1)),
            pl.BlockSpec((taps, tn), conv_col),
            pl.BlockSpec((1, tn), conv_col),
        ],
        out_specs=[pl.BlockSpec((t, tn), lambda i, j: (i, j)), pl.BlockSpec((t, n_extra), lambda i, j: (i, 0))],
        out_shape=[jax.ShapeDtypeStruct((m, n), BF16), jax.ShapeDtypeStruct((m, n_extra), F32)],
        scratch_shapes=[pltpu.VMEM((t, d), BF16), pltpu.VMEM((2, t + 2 * SUBLANES, tn // n_sub), F32)],
        compiler_params=_params("parallel", "arbitrary"),
        name="in_proj_conv",
    )(x, g.reshape(1, d).astype(F32), w, w, conv_w.astype(F32), conv_b.reshape(1, c).astype(F32))


def _dt_prep_kernel(raw_ref, bias_ref, alog_ref, dt_ref, acum_ref, *, chunk):
    t = raw_ref.shape[1]
    nh2 = raw_ref.shape[2]
    ri = lax.broadcasted_iota(jnp.int32, (chunk, chunk), 0)
    ci = lax.broadcasted_iota(jnp.int32, (chunk, chunk), 1)
    tri_f = (ci <= ri).astype(F32)
    tri_b = (ci >= ri).astype(F32)
    lane = lax.broadcasted_iota(jnp.int32, (chunk, nh2), 1)
    a = -jnp.exp(alog_ref[...])
    bias = bias_ref[...]

    def body(c, carry):
        rows = pl.ds(pl.multiple_of(c * chunk, chunk), chunk)
        dt = jax.nn.softplus(raw_ref[0, rows, :] + bias)
        dt_ref[0, rows, :] = dt
        da = dt * a
        pre = jnp.dot(tri_f, da, preferred_element_type=F32, precision=lax.Precision.HIGHEST)
        suf = jnp.dot(tri_b, da, preferred_element_type=F32, precision=lax.Precision.HIGHEST)
        acum_ref[0, rows, :] = jnp.where(lane < nh2 // 2, pre, suf)
        return carry

    lax.fori_loop(0, t // chunk, body, 0)


def dt_prep(dt_raw, dt_bias, a_log):
    b, t, nh2 = dt_raw.shape
    chunk = _pick_tile(t // 2, SSD_CHUNK)
    spec = pl.BlockSpec((1, t, nh2), lambda bi: (bi, 0, 0))
    vec = pl.BlockSpec((1, nh2), lambda bi: (0, 0))
    return pl.pallas_call(
        functools.partial(_dt_prep_kernel, chunk=chunk),
        grid=(b,),
        in_specs=[spec, vec, vec],
        out_specs=[spec, spec],
        out_shape=[jax.ShapeDtypeStruct((b, t, nh2), F32)] * 2,
        compiler_params=_params("parallel"),
        name="ssd_dt_prep",
    )(dt_raw, dt_bias.reshape(1, nh2).astype(F32), a_log.reshape(1, nh2).astype(F32))


def _ssd_kernel(x_ref, b_ref, c_ref, z_ref, acum_ref, dt_ref, arow_ref, drow_ref, dskip_ref, ng_ref,
                o_ref, yacc_ref, st_ref, *, chunk):
    t = x_ref.shape[1]
    width = x_ref.shape[2]
    npair = width // LANES
    hp = LANES // 2
    heads = 2 * npair
    nc = t // chunk
    half_lanes = acum_ref.shape[2] // 2
    grp = pl.program_id(1)
    ri = lax.broadcasted_iota(jnp.int32, (chunk, chunk), 0)
    ci = lax.broadcasted_iota(jnp.int32, (chunk, chunk), 1)
    lane = lax.broadcasted_iota(jnp.int32, (chunk, LANES), 1)
    lo = lane < hp
    hi_i = (lane >= hp).astype(jnp.int32)

    def lane_take(a, idx):
        return jnp.take_along_axis(a, idx, axis=1, mode="promise_in_bounds")

    def dir_chunk(d, c, last):
        mask = (ri >= ci) if d == 0 else (ri <= ci)
        rows = pl.ds(pl.multiple_of(c * chunk, chunk), chunk)
        x = x_ref[0, rows, :].astype(F32)
        bc = b_ref[0, rows, :]
        cc = c_ref[0, rows, :]
        a_all = acum_ref[0, rows, :] * LOG2E
        dt_all = dt_ref[0, rows, :]
        a_end = a_all[chunk - 1:chunk, :] if d == 0 else a_all[0:1, :]
        w_all = jnp.exp2(a_end - a_all) * dt_all
        ee_all = jnp.broadcast_to(jnp.exp2(a_end), (SUBLANES, a_all.shape[1]))
        arow = arow_ref[0, d, 0, c] * LOG2E - jnp.log2(drow_ref[0, d, 0, c])
        base = d * half_lanes + grp * heads

        cb = _dot_nt(cc, bc)
        coff = _dot(cc, st_ref[d].astype(BF16))

        ys, xws, ends = [], [], []
        for p in range(npair):
            pair_idx = base + 2 * p + hi_i
            xt = x[:, p * LANES:(p + 1) * LANES]
            a_cols = [lane_take(a_all, jnp.zeros_like(lane) + (base + 2 * p + half)) for half in range(2)]
            y = coff[:, p * LANES:(p + 1) * LANES] * jnp.exp2(jnp.where(lo, a_cols[0], a_cols[1]))
            for half in range(2):
                k = 2 * p + half
                seg = a_cols[half] - arow[k:k + 1, :]
                mk = (cb * jnp.exp2(jnp.where(mask, seg, NEG_BIG))).astype(BF16)
                keep = lo if half == 0 else jnp.logical_not(lo)
                y = y + _dot(mk, jnp.where(keep, xt, 0.0).astype(BF16))
            ys.append(y)
            xws.append((xt * lane_take(w_all, pair_idx)).astype(BF16))
            ends.append(lane_take(ee_all, pair_idx[:SUBLANES])[0:1])
        y = jnp.concatenate(ys, axis=-1)
        xw = jnp.concatenate(xws, axis=-1)
        e_row = jnp.concatenate(ends, axis=-1)
        st_ref[d] = st_ref[d] * e_row + _dot_tn(bc, xw)
        if not last:
            yacc_ref[rows, :] = y
        else:
            tot = yacc_ref[rows, :] + y + dskip_ref[...] * x
            tot = tot * _silu(z_ref[0, rows, :].astype(F32))
            o_ref[0, rows, :] = _rms(tot, ng_ref[...]).astype(o_ref.dtype)

    st_ref[...] = jnp.zeros_like(st_ref)
    half = nc // 2

    def scan(last):
        def body(i, carry):
            dir_chunk(0, i, last)
            dir_chunk(1, nc - 1 - i, last)
            return carry
        return body

    lax.fori_loop(0, half, scan(False), 0, unroll=2)
    lax.fori_loop(half, nc, scan(True), 0, unroll=2)


def ssd_scan(xbc, x_col, zx, acum, dt, arow, drow, d_skip_wide, norm_g):
    b, t, _ = xbc.shape
    g = SSD_GROUPS
    n = SSD_STATE
    inner = d_skip_wide.shape[0]
    width = inner // g
    kh = width // SSD_HEADDIM
    chunk = _pick_tile(t // 2, SSD_CHUNK)
    x_off = x_col // width
    b_off = (x_col + inner) // n
    c_off = b_off + g
    nh2 = acum.shape[2]
    all_heads = pl.BlockSpec((1, t, nh2), lambda bi, gi: (bi, 0, 0))
    row_spec = pl.BlockSpec((1, 2, 1, t // chunk, kh, chunk), lambda bi, gi: (bi, 0, gi, 0, 0, 0))
    return pl.pallas_call(
        functools.partial(_ssd_kernel, chunk=chunk),
        grid=(b, g),
        in_specs=[
            pl.BlockSpec((1, t, width), lambda bi, gi: (bi, 0, x_off + gi)),
            pl.BlockSpec((1, t, n), lambda bi, gi: (bi, 0, b_off + gi)),
            pl.BlockSpec((1, t, n), lambda bi, gi: (bi, 0, c_off + gi)),
            pl.BlockSpec((1, t, width), lambda bi, gi: (bi, 0, gi)),
            all_heads, all_heads, row_spec, row_spec,
            pl.BlockSpec((1, width), lambda bi, gi: (0, gi)),
            pl.BlockSpec((1, width), lambda bi, gi: (0, gi)),
        ],
        out_specs=pl.BlockSpec((1, t, width), lambda bi, gi: (bi, 0, gi)),
        out_shape=jax.ShapeDtypeStruct((b, t, inner), BF16),
        scratch_shapes=[pltpu.VMEM((t, width), F32), pltpu.VMEM((2, n, width), F32)],
        compiler_params=_params("parallel", "arbitrary"),
        name="ssd_scan",
    )(xbc, xbc, xbc, zx, acum, dt, arow, drow,
      d_skip_wide.reshape(1, inner).astype(F32), norm_g.reshape(1, inner).astype(F32))


def _even_mixer(xf, b, t, gn, w_in, q_gain, k_gain, w_out):
    d = xf.shape[1]
    proj = norm_matmul(xf, gn[0], w_in).reshape(b, t, -1)
    heads = jnp.arange(RET_HEADS, dtype=F32)
    lg = jnp.stack([jnp.log1p(-jnp.exp2(-(RET_DECAY_EXP_FWD + heads))),
                    jnp.log1p(-jnp.exp2(-(RET_DECAY_EXP_BWD + heads)))], axis=1)
    cos1, sin1 = _retention_tables(t)
    ret = retention(proj, lg, cos1, sin1)
    cos2, sin2 = _axial_tables(t)
    att_base = 2 * RET_HEADS * RET_DK + 2 * RET_HEADS * RET_DV
    gw = ATT_GROUP * ATT_HD
    k_base = att_base + ATT_HEADS * ATT_HD
    v_base = k_base + ATT_KV_HEADS * ATT_HD
    att = gqa_attention(proj, att_base // gw, k_base // ATT_HD, v_base // ATT_HD,
                        q_gain, k_gain, cos2, sin2)
    parts = [ret.reshape(b * t, -1), att.reshape(b * t, -1)]
    return matmul_norm_resid(parts, w_out, gn[1], xf)


def _odd_mixer(xf, b, t, gn, w_in, conv_w, conv_b, a_log, dt_bias, d_skip, norm_g, w_out):
    inner = norm_g.shape[0]
    nh = d_skip.shape[0]
    g = SSD_GROUPS
    kh = nh // g
    zx, dt_raw = in_proj_conv(xf, gn[0], w_in, conv_w, conv_b, t, inner, 2 * nh)
    zx = zx.reshape(b, t, -1)
    dt_raw = dt_raw.reshape(b, t, 2 * nh)
    dt, acum = dt_prep(dt_raw, dt_bias.reshape(-1), a_log.reshape(-1))
    chunk = _pick_tile(t // 2, SSD_CHUNK)
    rows_of = lambda arr: arr.reshape(b, t // chunk, chunk, 2, g, kh).transpose(0, 3, 4, 1, 5, 2)
    y = ssd_scan(zx, inner, zx, acum, dt, rows_of(acum), rows_of(dt),
                 jnp.repeat(d_skip.astype(F32), inner // nh), norm_g)
    return matmul_norm_resid([y.reshape(b * t, inner)], w_out, gn[1], xf)


def _trunk(x, mem, norm_g, wts):
    b, t, d = x.shape
    nm = mem.shape[1]
    xf = x.reshape(b * t, d)
    memf = mem.reshape(b * nm, d)
    depth = norm_g.shape[0]
    for i in range(depth):
        gn = norm_g[i]
        j = i // 2
        if i % 2 == 0:
            xf = _even_mixer(xf, b, t, gn, wts["ev_w_in"][j], wts["ev_q_gain"][j], wts["ev_k_gain"][j],
                             wts["ev_w_out"][j])
        else:
            xf = _odd_mixer(xf, b, t, gn, wts["od_w_in"][j], wts["od_conv_w"][j],
                            wts["od_conv_b"][j], wts["od_a_log"][j], wts["od_dt_bias"][j], wts["od_d"][j],
                            wts["od_norm_g"][j], wts["od_w_out"][j])
        q = norm_matmul(xf, gn[2], wts["xa_wq"][i]).reshape(b, t, d)
        kv = norm_matmul(memf, gn[4], wts["xa_wkv"][i]).reshape(b, nm, 2 * d)
        xa = cross_attention(q, kv).reshape(b * t, d)
        xf = matmul_norm_resid([xa], wts["xa_wo"][i], gn[3], xf)
        act = norm_swiglu(xf, gn[5], wts["ffn_w_gu"][i])
        xf = matmul_norm_resid(act, wts["ffn_w_down"][i], gn[6], xf)
    return xf.reshape(b, t, d)


def kernel(x_prompt, x_sample, mem_prompt, mem_sample, norm_g, ev_w_in, ev_q_gain, ev_k_gain, ev_w_out,
           od_w_in, od_conv_w, od_conv_b, od_a_log, od_dt_bias, od_d, od_norm_g, od_w_out,
           xa_wq, xa_wkv, xa_wo, ffn_w_gu, ffn_w_down):
    wts = dict(
        ev_w_in=ev_w_in.astype(BF16), ev_q_gain=ev_q_gain, ev_k_gain=ev_k_gain,
        ev_w_out=ev_w_out.astype(BF16),
        od_w_in=od_w_in.astype(BF16),
        od_conv_w=od_conv_w, od_conv_b=od_conv_b, od_a_log=od_a_log, od_dt_bias=od_dt_bias,
        od_d=od_d, od_norm_g=od_norm_g, od_w_out=od_w_out.astype(BF16),
        xa_wq=xa_wq.astype(BF16), xa_wkv=xa_wkv.astype(BF16), xa_wo=xa_wo.astype(BF16),
        ffn_w_gu=ffn_w_gu.astype(BF16), ffn_w_down=ffn_w_down.astype(BF16),
    )
    y_prompt = _trunk(x_prompt, mem_prompt, norm_g, wts)
    y_sample = _trunk(x_sample, mem_sample, norm_g, wts)
    return (y_prompt, y_sample)
```
